```python
import jax, jax.numpy as jnp
from jax import lax
import numpy as np

D_MODEL = 1024
BATCH = 16
SEQ = 2048
DEPTH = 2

N_MIXERS = 2
N_RET_LAYERS = (DEPTH + 1) // 2
N_MLA_LAYERS = DEPTH // 2

D_FF = 4 * D_MODEL
RMS_EPS = 1e-6
GN_EPS = 1e-5
ROPE_THETA = 10000.0
N_ADA = 6

RET_HEADS = D_MODEL // 256
RET_DK = D_MODEL // RET_HEADS
RET_DV = 2 * RET_DK
RET_CHUNK = 128
RET_IN_COLS = 2 * RET_HEADS * RET_DK + 2 * RET_HEADS * RET_DV

MLA_HEADS = D_MODEL // 128
MLA_NOPE = 128
MLA_ROPE = 64
MLA_V = 128
MLA_DQK = MLA_NOPE + MLA_ROPE
MLA_Q_LORA = 3 * D_MODEL // 8
MLA_KV_LORA = D_MODEL // 4
MLA_IN_COLS = MLA_Q_LORA + MLA_KV_LORA + MLA_ROPE
MLA_QBLOCK = 128

kernel_name = "bidir_retention_mla_interleaved_sqrelu_adaln"

F32 = jnp.float32


def rmsnorm(t, w):
    t32 = t.astype(F32)
    y = t32 * lax.rsqrt(jnp.mean(t32 * t32, axis=-1, keepdims=True) + RMS_EPS)
    return (y * w.astype(F32)).astype(t.dtype)


def rope(t, positions):
    d = t.shape[-1]
    inv = ROPE_THETA ** (-jnp.arange(0, d, 2, dtype=F32) / d)
    ang = positions.astype(F32)[..., None] * inv
    cos = jnp.cos(ang)[:, :, None, :]
    sin = jnp.sin(ang)[:, :, None, :]
    t32 = t.astype(F32)
    t1, t2 = t32[..., : d // 2], t32[..., d // 2:]
    return jnp.concatenate([t1 * cos - t2 * sin, t1 * sin + t2 * cos], axis=-1).astype(t.dtype)


def retention_one_direction(q, k, v, log_gamma, strict):
    b, s, h, dk = q.shape
    dv = v.shape[-1]
    n_chunks = s // RET_CHUNK

    def chunks(t):
        return t.reshape(b, n_chunks, RET_CHUNK, h, t.shape[-1]).transpose(1, 0, 3, 2, 4)

    qc, kc, vc = chunks(q), chunks(k), chunks(v)
    pos = jnp.arange(RET_CHUNK, dtype=F32)
    diff = pos[:, None] - pos[None, :]
    mask = (diff > 0) if strict else (diff >= 0)
    lg = log_gamma.astype(F32)
    intra_decay = jnp.where(mask[None], jnp.exp(jnp.where(mask[None], diff[None] * lg[:, None, None], 0.0)), 0.0)
    cross_decay = jnp.exp((pos + 1.0)[None, :] * lg[:, None])[None, :, :, None]
    state_decay = jnp.exp((RET_CHUNK - 1.0 - pos)[None, :] * lg[:, None])[None, :, :, None]
    chunk_decay = jnp.exp(RET_CHUNK * lg)[None, :, None, None]

    def step(state, inp):
        qi, ki, vi = inp
        scores = jnp.einsum('bhnd,bhmd->bhnm', qi, ki) * intra_decay
        out = jnp.einsum('bhnm,bhmv->bhnv', scores, vi)
        out = out + jnp.einsum('bhnd,bhdv->bhnv', qi, state) * cross_decay
        state = state * chunk_decay + jnp.einsum('bhmd,bhmv->bhdv', ki * state_decay, vi)
        return state, out

    state0 = jnp.zeros((b, h, dk, dv), F32)
    _, out = lax.scan(step, state0, (qc, kc, vc))
    return out.transpose(1, 0, 3, 2, 4).reshape(b, s, h, dv)


def retention_mixer(hdn, positions, w_in, logit_fwd, logit_bwd, gn_w, w_out):
    b, s, _ = hdn.shape
    hk = RET_HEADS * RET_DK
    hv = RET_HEADS * RET_DV
    proj = hdn @ w_in
    q, k, v, g = jnp.split(proj, [hk, 2 * hk, 2 * hk + hv], axis=-1)
    q = rope(q.reshape(b, s, RET_HEADS, RET_DK), positions).astype(F32)
    k = rope(k.reshape(b, s, RET_HEADS, RET_DK), positions).astype(F32) * (RET_DK ** -0.5)
    v = v.reshape(b, s, RET_HEADS, RET_DV).astype(F32)
    lg_f = jax.nn.log_sigmoid(logit_fwd.astype(F32))
    lg_b = jax.nn.log_sigmoid(logit_bwd.astype(F32))
    y_f = retention_one_direction(q, k, v, lg_f, strict=False)
    y_b = retention_one_direction(q[:, ::-1], k[:, ::-1], v[:, ::-1], lg_b, strict=True)[:, ::-1]
    y = y_f + y_b
    mu = jnp.mean(y, axis=-1, keepdims=True)
    var = jnp.mean(jnp.square(y - mu), axis=-1, keepdims=True)
    y = ((y - mu) * lax.rsqrt(var + GN_EPS)).reshape(b, s, hv) * gn_w.astype(F32)
    y = jax.nn.silu(g.astype(F32)) * y
    return y.astype(hdn.dtype) @ w_out


def mla_mixer(hdn, positions, w_in, q_norm, w_uq, kv_norm, w_ukv, w_out):
    b, s, _ = hdn.shape
    c_q, c_kv, k_rope = jnp.split(hdn @ w_in, [MLA_Q_LORA, MLA_Q_LORA + MLA_KV_LORA], axis=-1)
    q = (rmsnorm(c_q, q_norm) @ w_uq).reshape(b, s, MLA_HEADS, MLA_DQK)
    q = jnp.concatenate([q[..., :MLA_NOPE], rope(q[..., MLA_NOPE:], positions)], axis=-1)
    kv = (rmsnorm(c_kv, kv_norm) @ w_ukv).reshape(b, s, MLA_HEADS, MLA_NOPE + MLA_V)
    k_nope, v = kv[..., :MLA_NOPE], kv[..., MLA_NOPE:]
    k_rope = rope(k_rope[:, :, None, :], positions)
    k = jnp.concatenate([k_nope, jnp.broadcast_to(k_rope, (b, s, MLA_HEADS, MLA_ROPE))], axis=-1)
    scale = MLA_DQK ** -0.5
    n_blk = s // MLA_QBLOCK
    q_blocks = q.reshape(b, n_blk, MLA_QBLOCK, MLA_HEADS, MLA_DQK).transpose(1, 0, 2, 3, 4)

    def attend(qi):
        sc = jnp.einsum('bqhd,bkhd->bhqk', qi, k, preferred_element_type=F32) * scale
        p = jax.nn.softmax(sc, axis=-1)
        return jnp.einsum('bhqk,bkhd->bqhd', p.astype(v.dtype), v)

    o = lax.map(attend, q_blocks)
    o = o.transpose(1, 0, 2, 3, 4).reshape(b, s, MLA_HEADS * MLA_V)
    return o @ w_out


def setup_inputs(seed: int = 0) -> dict:
    key = jax.random.key(seed)
    ks = jax.random.split(key, 24)

    def w(k, shape, fan_in, s=1.0):
        return jax.random.normal(k, shape, F32) * (s * fan_in ** -0.5)

    def gain(k, shape):
        return 1.0 + 0.05 * jax.random.normal(k, shape, F32)

    base_logit = jnp.asarray(np.log(2.0 ** (5.0 + np.arange(RET_HEADS)) - 1.0), F32)
    x = jax.random.normal(ks[0], (BATCH, SEQ, D_MODEL), F32)
    c = jax.random.normal(ks[1], (BATCH, D_MODEL), F32)
    offsets = jax.random.randint(ks[2], (BATCH, 1), 0, 4096, dtype=jnp.int32)
    positions = offsets + jnp.arange(SEQ, dtype=jnp.int32)[None, :]
    return {
        "x": x,
        "c": c,
        "positions": positions,
        "norm_w": gain(ks[3], (DEPTH, 4, D_MODEL)),
        "ada_w": w(ks[4], (DEPTH, D_MODEL, N_ADA * D_MODEL), D_MODEL, 0.5),
        "ada_b": 0.02 * jax.random.normal(ks[5], (DEPTH, N_ADA * D_MODEL), F32),
        "ret_w_in": w(ks[6], (N_RET_LAYERS, D_MODEL, RET_IN_COLS), D_MODEL),
        "ret_decay_logit_fwd": base_logit + 0.1 * jax.random.normal(ks[7], (N_RET_LAYERS, RET_HEADS), F32),
        "ret_decay_logit_bwd": base_logit + 0.1 * jax.random.normal(ks[8], (N_RET_LAYERS, RET_HEADS), F32),
        "ret_gn_w": gain(ks[9], (N_RET_LAYERS, RET_HEADS * RET_DV)),
        "ret_w_out": w(ks[10], (N_RET_LAYERS, RET_HEADS * RET_DV, D_MODEL), RET_HEADS * RET_DV),
        "mla_w_in": w(ks[11], (N_MLA_LAYERS, D_MODEL, MLA_IN_COLS), D_MODEL),
        "mla_q_norm": gain(ks[12], (N_MLA_LAYERS, MLA_Q_LORA)),
        "mla_w_uq": w(ks[13], (N_MLA_LAYERS, MLA_Q_LORA, MLA_HEADS * MLA_DQK), MLA_Q_LORA),
        "mla_kv_norm": gain(ks[14], (N_MLA_LAYERS, MLA_KV_LORA)),
        "mla_w_ukv": w(ks[15], (N_MLA_LAYERS, MLA_KV_LORA, MLA_HEADS * (MLA_NOPE + MLA_V)), MLA_KV_LORA),
        "mla_w_out": w(ks[16], (N_MLA_LAYERS, MLA_HEADS * MLA_V, D_MODEL), MLA_HEADS * MLA_V),
        "mlp_w1": w(ks[17], (DEPTH, D_MODEL, D_FF), D_MODEL),
        "mlp_w2": w(ks[18], (DEPTH, D_FF, D_MODEL), D_FF),
    }


def reference(x, c, positions, norm_w, ada_w, ada_b, ret_w_in, ret_decay_logit_fwd,
              ret_decay_logit_bwd, ret_gn_w, ret_w_out, mla_w_in, mla_q_norm, mla_w_uq,
              mla_kv_norm, mla_w_ukv, mla_w_out, mlp_w1, mlp_w2):
    c_act = jax.nn.silu(c)
    for i in range(DEPTH):
        mod = c_act @ ada_w[i] + ada_b[i]
        sh_a, sc_a, g_a, sh_m, sc_m, g_m = [m[:, None, :] for m in jnp.split(mod, N_ADA, axis=-1)]
        hdn = rmsnorm(x, norm_w[i, 0]) * (1.0 + sc_a) + sh_a
        j = i // N_MIXERS
        if i % N_MIXERS == 0:
            y = retention_mixer(hdn, positions, ret_w_in[j], ret_decay_logit_fwd[j],
                                ret_decay_logit_bwd[j], ret_gn_w[j], ret_w_out[j])
        else:
            y = mla_mixer(hdn, positions, mla_w_in[j], mla_q_norm[j], mla_w_uq[j],
                          mla_kv_norm[j], mla_w_ukv[j], mla_w_out[j])
        x = x + g_a * rmsnorm(y, norm_w[i, 1])
        hdn = rmsnorm(x, norm_w[i, 2]) * (1.0 + sc_m) + sh_m
        y = jnp.square(jax.nn.relu(hdn @ mlp_w1[i])) @ mlp_w2[i]
        x = x + g_m * rmsnorm(y, norm_w[i, 3])
    return x
```

```python
import functools

import numpy as np
import jax
import jax.numpy as jnp
from jax import lax
from jax.experimental import pallas as pl
from jax.experimental.pallas import tpu as pltpu

F32 = jnp.float32
BF16 = jnp.bfloat16

RMS_EPS = 1e-6
GN_EPS = 1e-5
ROPE_THETA = 10000.0
N_ADA = 6

RET_DK = 256
RET_DV = 512
RET_CHUNK = 128

MLA_NOPE = 128
MLA_ROPE = 64
MLA_V = 128
MLA_DQK = MLA_NOPE + MLA_ROPE
MLA_HEAD_PAD = 256

LANES = 128
MIB = 1024 * 1024


def _cparams(semantics, vmem_mib):
    return pltpu.CompilerParams(dimension_semantics=semantics,
                                vmem_limit_bytes=vmem_mib * MIB)


def _rms(t, w):
    return t * lax.rsqrt(jnp.mean(t * t, axis=-1, keepdims=True) + RMS_EPS) * w


def _dot(a, b):
    return jnp.dot(a, b, preferred_element_type=F32)


def _dot_nt(a, b):
    return lax.dot_general(a, b, (((1,), (1,)), ((), ())), preferred_element_type=F32)


def _dot_tn(a, b):
    return lax.dot_general(a, b, (((0,), (0,)), ((), ())), preferred_element_type=F32)


def _ada_kernel(c_ref, w_ref, b_ref, o_ref):
    c_act = jax.nn.silu(c_ref[...])
    o_ref[...] = _dot(c_act.astype(BF16), w_ref[...].astype(BF16)) + b_ref[...]


def _ada_mod(c, ada_w, ada_b):
    depth, d, n = ada_w.shape
    b = c.shape[0]
    tn = 1024
    return pl.pallas_call(
        _ada_kernel,
        grid=(depth, n // tn),
        in_specs=[
            pl.BlockSpec((b, d), lambda l, j: (0, 0)),
            pl.BlockSpec((None, d, tn), lambda l, j: (l, 0, j)),
            pl.BlockSpec((None, 1, tn), lambda l, j: (l, 0, j)),
        ],
        out_specs=pl.BlockSpec((None, b, tn), lambda l, j: (l, 0, j)),
        out_shape=jax.ShapeDtypeStruct((depth, b, n), F32),
        compiler_params=_cparams(("arbitrary", "arbitrary"), 32),
        name="ada_mod",
    )(c, ada_w, ada_b.reshape(depth, 1, n))


def _rope_kernel(pos_ref, inv_r_ref, inv_m_ref, sign_m_ref, cr_ref, sr_ref, cm_ref, sm_ref):
    pos = pos_ref[...].astype(F32)
    ang_r = pos * inv_r_ref[...]
    cr_ref[...] = jnp.cos(ang_r)
    sr_ref[...] = jnp.sin(ang_r)
    ang_m = pos * inv_m_ref[...]
    cm_ref[...] = jnp.cos(ang_m)
    sm_ref[...] = jnp.sin(ang_m) * sign_m_ref[...]


def _rope_tables(positions):
    t = positions.size
    ts = 2048
    half_r = RET_DK // 2
    half_m = MLA_ROPE // 2
    inv_r = ROPE_THETA ** (-jnp.arange(0, RET_DK, 2, dtype=F32) / RET_DK)
    inv_m = ROPE_THETA ** (-jnp.arange(0, MLA_ROPE, 2, dtype=F32) / MLA_ROPE)
    reps = LANES // half_m
    inv_m = jnp.tile(inv_m, reps)
    sign_m = jnp.tile(jnp.concatenate([-jnp.ones((half_m,), F32), jnp.ones((half_m,), F32)]), reps // 2)
    tab = jax.ShapeDtypeStruct((t, LANES), F32)
    vec = pl.BlockSpec((1, LANES), lambda i: (0, 0))
    out = pl.BlockSpec((ts, LANES), lambda i: (i, 0))
    assert half_r == LANES
    return pl.pallas_call(
        _rope_kernel,
        grid=(t // ts,),
        in_specs=[pl.BlockSpec((ts, 1), lambda i: (i, 0)), vec, vec, vec],
        out_specs=[out, out, out, out],
        out_shape=[tab, tab, tab, tab],
        compiler_params=_cparams(("arbitrary",), 32),
        name="rope_tables",
    )(positions.reshape(t, 1), inv_r.reshape(1, LANES), inv_m.reshape(1, LANES), sign_m.reshape(1, LANES))


def _ret_proj_kernel(x_ref, nw_ref, sc_ref, sh_ref, w_ref, cos_ref, sin_ref, o_ref, h_scr,
                     *, k_scale):
    j = pl.program_id(1)

    @pl.when(j == 0)
    def _():
        hn = _rms(x_ref[...], nw_ref[...])
        h_scr[...] = (hn * (1.0 + sc_ref[...]) + sh_ref[...]).astype(BF16)

    p = _dot(h_scr[...], w_ref[...])
    tn = p.shape[1]

    @pl.when(j < 2)
    def _():
        scale = jnp.where(j == 1, k_scale, 1.0).astype(F32)
        cos = cos_ref[...] * scale
        sin = sin_ref[...] * scale
        half = RET_DK // 2
        for g in range(tn // RET_DK):
            t1 = p[:, g * RET_DK:g * RET_DK + half]
            t2 = p[:, g * RET_DK + half:(g + 1) * RET_DK]
            o_ref[:, g * RET_DK:g * RET_DK + half] = (t1 * cos - t2 * sin).astype(BF16)
            o_ref[:, g * RET_DK + half:(g + 1) * RET_DK] = (t1 * sin + t2 * cos).astype(BF16)

    @pl.when(j >= 2)
    def _():
        o_ref[...] = p.astype(BF16)


def _ret_proj(x2, nw, mod, layer, w_in, cos_r, sin_r, seq):
    t, d = x2.shape
    n = w_in.shape[1]
    heads_dk = (n // 6)
    tm, tn = 1024, 1024
    assert heads_dk == tn
    per_b = seq // tm
    vec = lambda k: pl.BlockSpec((None, None, 1, d), lambda i, j: (layer, i // per_b, 0, k))
    return pl.pallas_call(
        functools.partial(_ret_proj_kernel, k_scale=RET_DK ** -0.5),
        grid=(t // tm, n // tn),
        in_specs=[
            pl.BlockSpec((tm, d), lambda i, j: (i, 0)),
            pl.BlockSpec((None, None, 1, d), lambda i, j: (layer, 0, 0, 0)),
            vec(1), vec(0),
            pl.BlockSpec((d, tn), lambda i, j: (0, j)),
            pl.BlockSpec((tm, LANES), lambda i, j: (i, 0)),
            pl.BlockSpec((tm, LANES), lambda i, j: (i, 0)),
        ],
        out_specs=pl.BlockSpec((tm, tn), lambda i, j: (i, j)),
        out_shape=jax.ShapeDtypeStruct((t, n), BF16),
        scratch_shapes=[pltpu.VMEM((tm, d), BF16)],
        compiler_params=_cparams(("parallel", "arbitrary"), 48),
        name="ret_proj",
    )(x2, nw, mod, mod, w_in, cos_r, sin_r)


def _retention_kernel(lf_ref, lb_ref, q_ref, k_ref, v_ref, g_ref, gn_ref, o_ref,
                      y_scr, sf_scr, sb_scr):
    c = RET_CHUNK
    n = q_ref.shape[0] // c
    lg_f = jax.nn.log_sigmoid(lf_ref[...].astype(F32))
    lg_b = jax.nn.log_sigmoid(lb_ref[...].astype(F32))
    row = lax.broadcasted_iota(jnp.int32, (c, c), 0).astype(F32)
    col = lax.broadcasted_iota(jnp.int32, (c, c), 1).astype(F32)
    pos = lax.broadcasted_iota(jnp.int32, (c, 1), 0).astype(F32)
    diff = row - col
    mask_f = diff >= 0
    mask_b = diff < 0
    decay_f = jnp.where(mask_f, jnp.exp(jnp.where(mask_f, diff * lg_f, 0.0)), 0.0)
    decay_b = jnp.where(mask_b, jnp.exp(jnp.where(mask_b, -diff * lg_b, 0.0)), 0.0)
    cross_f = jnp.exp((pos + 1.0) * lg_f)
    cross_b = jnp.exp((c - pos) * lg_b)
    sdec_f = jnp.exp((c - 1.0 - pos) * lg_f)
    sdec_b = jnp.exp(pos * lg_b)
    cdec_f = jnp.exp(c * lg_f)
    cdec_b = jnp.exp(c * lg_b)
    gn_w = gn_ref[...]

    sf_scr[...] = jnp.zeros_like(sf_scr)
    sb_scr[...] = jnp.zeros_like(sb_scr)

    def rows(ci):
        return pl.ds(pl.multiple_of(ci * c, c), c)

    def step(ci, st_ref, decay, cross, sdec, cdec):
        r = rows(ci)
        qi, ki, vi = q_ref[r, :], k_ref[r, :], v_ref[r, :]
        s = _dot_nt(qi, ki) * decay
        out = _dot(s.astype(BF16), vi)
        st = st_ref[...]
        out = out + _dot(qi, st.astype(BF16)) * cross
        kd = (ki.astype(F32) * sdec).astype(BF16)
        st_ref[...] = st * cdec + _dot_tn(kd, vi)
        return out

    def finalize(ci, y):
        r = rows(ci)
        mu = jnp.mean(y, axis=-1, keepdims=True)
        yc = y - mu
        var = jnp.mean(yc * yc, axis=-1, keepdims=True)
        yn = yc * lax.rsqrt(var + GN_EPS) * gn_w
        o_ref[r, :] = (jax.nn.silu(g_ref[r, :].astype(F32)) * yn).astype(BF16)

    def first_half(i, carry):
        cb = n - 1 - i
        y_scr[rows(i), :] = step(i, sf_scr, decay_f, cross_f, sdec_f, cdec_f)
        y_scr[rows(cb), :] = step(cb, sb_scr, decay_b, cross_b, sdec_b, cdec_b)
        return carry

    def second_half(i, carry):
        cb = n - 1 - i
        finalize(i, y_scr[rows(i), :] + step(i, sf_scr, decay_f, cross_f, sdec_f, cdec_f))
        finalize(cb, y_scr[rows(cb), :] + step(cb, sb_scr, decay_b, cross_b, sdec_b, cdec_b))
        return carry

    lax.fori_loop(0, n // 2, first_half, 0)
    lax.fori_loop(n // 2, n, second_half, 0)


def _retention(proj, logit_f, logit_b, gn_w, batch, seq, heads):
    t = proj.shape[0]
    hk = heads * RET_DK
    hv = heads * RET_DV
    assert (seq // RET_CHUNK) % 2 == 0
    kb, vb, gb = hk // RET_DK, 2 * hk // RET_DV, (2 * hk + hv) // RET_DV
    scalar = pl.BlockSpec((None, 1, 1), lambda b, h: (h, 0, 0))
    return pl.pallas_call(
        _retention_kernel,
        grid=(batch, heads),
        in_specs=[
            scalar, scalar,
            pl.BlockSpec((seq, RET_DK), lambda b, h: (b, h)),
            pl.BlockSpec((seq, RET_DK), lambda b, h: (b, kb + h)),
            pl.BlockSpec((seq, RET_DV), lambda b, h: (b, vb + h)),
            pl.BlockSpec((seq, RET_DV), lambda b, h: (b, gb + h)),
            pl.BlockSpec((1, RET_DV), lambda b, h: (0, h)),
        ],
        out_specs=pl.BlockSpec((seq, RET_DV), lambda b, h: (b, h)),
        out_shape=jax.ShapeDtypeStruct((t, hv), BF16),
        scratch_shapes=[
            pltpu.VMEM((seq, RET_DV), F32),
            pltpu.VMEM((RET_DK, RET_DV), F32),
            pltpu.VMEM((RET_DK, RET_DV), F32),
        ],
        compiler_params=_cparams(("parallel", "parallel"), 48),
        name="retention",
    )(logit_f.reshape(heads, 1, 1), logit_b.reshape(heads, 1, 1), proj, proj, proj, proj,
      gn_w.reshape(1, hv))


def _outproj_kernel(a_ref, w_ref, x_ref, gate_ref, nw_post_ref, nw_pre_ref, sc_ref, sh_ref,
                    xo_ref, h_ref):
    y = _dot(a_ref[...], w_ref[...])
    x = x_ref[...] + gate_ref[...] * _rms(y, nw_post_ref[...])
    xo_ref[...] = x
    h_ref[...] = (_rms(x, nw_pre_ref[...]) * (1.0 + sc_ref[...]) + sh_ref[...]).astype(BF16)


def _outproj(a, w_out, x2, nw, mod, layer, seq):
    t, d = x2.shape
    kd = a.shape[1]
    tm = 512
    per_b = seq // tm
    vec = lambda k: pl.BlockSpec((None, None, 1, d), lambda i: (layer, i // per_b, 0, k))
    nvec = lambda k: pl.BlockSpec((None, None, 1, d), lambda i: (layer, k, 0, 0))
    row = pl.BlockSpec((tm, d), lambda i: (i, 0))
    return pl.pallas_call(
        _outproj_kernel,
        grid=(t // tm,),
        in_specs=[
            pl.BlockSpec((tm, kd), lambda i: (i, 0)),
            pl.BlockSpec((kd, d), lambda i: (0, 0)),
            row, vec(2), nvec(1), nvec(2), vec(4), vec(3),
        ],
        out_specs=[row, row],
        out_shape=[jax.ShapeDtypeStruct((t, d), F32), jax.ShapeDtypeStruct((t, d), BF16)],
        compiler_params=_cparams(("parallel",), 48),
        name="mixer_outproj",
    )(a, w_out, x2, mod, nw, nw, mod, mod)


def _mlp_kernel(*refs, emit_next):
    if emit_next:
        (h_ref, w1_ref, w2_ref, x_ref, gate_ref, nw_post_ref, nw_pre_ref, sc_ref, sh_ref,
         xo_ref, hn_ref, acc_scr) = refs
    else:
        h_ref, w1_ref, w2_ref, x_ref, gate_ref, nw_post_ref, xo_ref, acc_scr = refs
    j = pl.program_id(1)

    @pl.when(j == 0)
    def _():
        acc_scr[...] = jnp.zeros_like(acc_scr)

    u = jnp.square(jnp.maximum(_dot(h_ref[...], w1_ref[...]), 0.0))
    acc_scr[...] += _dot(u.astype(BF16), w2_ref[...])

    @pl.when(j == pl.num_programs(1) - 1)
    def _():
        x = x_ref[...] + gate_ref[...] * _rms(acc_scr[...], nw_post_ref[...])
        xo_ref[...] = x
        if emit_next:
            hn = _rms(x, nw_pre_ref[...]) * (1.0 + sc_ref[...]) + sh_ref[...]
            hn_ref[...] = hn.astype(BF16)


def _mlp(h, w1, w2, x2, nw, mod, layer, seq, emit_next):
    t, d = x2.shape
    dff = w1.shape[1]
    tm, tf = 512, 1024
    per_b = seq // tm
    vec = lambda l, k: pl.BlockSpec((None, None, 1, d), lambda i, j: (l, i // per_b, 0, k))
    nvec = lambda l, k: pl.BlockSpec((None, None, 1, d), lambda i, j: (l, k, 0, 0))
    row = pl.BlockSpec((tm, d), lambda i, j: (i, 0))
    in_specs = [
        row,
        pl.BlockSpec((d, tf), lambda i, j: (0, j)),
        pl.BlockSpec((tf, d), lambda i, j: (j, 0)),
        row, vec(layer, 5), nvec(layer, 3),
    ]
    args = [h, w1, w2, x2, mod, nw]
    out_specs = [row]
    out_shape = [jax.ShapeDtypeStruct((t, d), F32)]
    if emit_next:
        in_specs += [nvec(layer + 1, 0), vec(layer + 1, 1), vec(layer + 1, 0)]
        args += [nw, mod, mod]
        out_specs.append(row)
        out_shape.append(jax.ShapeDtypeStruct((t, d), BF16))
    return pl.pallas_call(
        functools.partial(_mlp_kernel, emit_next=emit_next),
        grid=(t // tm, dff // tf),
        in_specs=in_specs,
        out_specs=out_specs,
        out_shape=out_shape,
        scratch_shapes=[pltpu.VMEM((tm, d), F32)],
        compiler_params=_cparams(("parallel", "arbitrary"), 48),
        name="mlp_next" if emit_next else "mlp_last",
    )(*args)


def _mla_proj_kernel(h_ref, wcq_ref, wckv_ref, wkr_ref, qn_ref, kvn_ref, wuq_ref, wukv_ref,
                     cos_ref, sin_ref, q_ref, k_ref, v_ref, *, heads, scale):
    h = h_ref[...]
    cos = cos_ref[...]
    sin = sin_ref[...]
    cq = _rms(_dot(h, wcq_ref[...]), qn_ref[...]).astype(BF16)
    ckv = _rms(_dot(h, wckv_ref[...]), kvn_ref[...]).astype(BF16)
    kr = _dot(h, wkr_ref[...])
    kr = (kr[:, :LANES] * cos + kr[:, LANES:] * sin).astype(BF16)
    nope_w = heads * MLA_NOPE
    qcos = cos * scale
    qsin = sin * scale
    for hd in range(heads):
        lo = hd * MLA_HEAD_PAD
        w_n = wuq_ref[:, hd * LANES:(hd + 1) * LANES]
        w_r = wuq_ref[:, nope_w + hd * LANES:nope_w + (hd + 1) * LANES]
        w_s = wuq_ref[:, 2 * nope_w + hd * LANES:2 * nope_w + (hd + 1) * LANES]
        q_ref[:, lo:lo + LANES] = (_dot(cq, w_n) * scale).astype(BF16)
        q_ref[:, lo + LANES:lo + 2 * LANES] = (_dot(cq, w_r) * qcos + _dot(cq, w_s) * qsin).astype(BF16)
        k_ref[:, lo:lo + LANES] = _dot(ckv, wukv_ref[:, hd * LANES:(hd + 1) * LANES]).astype(BF16)
        k_ref[:, lo + LANES:lo + 2 * LANES] = kr
    v_ref[...] = _dot(ckv, wukv_ref[:, nope_w:]).astype(BF16)


def _mla_proj(h, wcq, wckv, wkr, q_norm, kv_norm, wuq, wukv, cos_m, sin_m, heads):
    t, d = h.shape
    tm = 512
    full = lambda a: pl.BlockSpec(a.shape, lambda i: (0,) * a.ndim)
    qn = q_norm.reshape(1, -1)
    kvn = kv_norm.reshape(1, -1)
    row = lambda w: pl.BlockSpec((tm, w), lambda i: (i, 0))
    return pl.pallas_call(
        functools.partial(_mla_proj_kernel, heads=heads, scale=MLA_DQK ** -0.5),
        grid=(t // tm,),
        in_specs=[row(d), full(wcq), full(wckv), full(wkr), full(qn), full(kvn), full(wuq),
                  full(wukv), row(LANES), row(LANES)],
        out_specs=[row(heads * MLA_HEAD_PAD), row(heads * MLA_HEAD_PAD), row(heads * MLA_V)],
        out_shape=[jax.ShapeDtypeStruct((t, heads * MLA_HEAD_PAD), BF16),
                   jax.ShapeDtypeStruct((t, heads * MLA_HEAD_PAD), BF16),
                   jax.ShapeDtypeStruct((t, heads * MLA_V), BF16)],
        compiler_params=_cparams(("parallel",), 48),
        name="mla_proj",
    )(h, wcq, wckv, wkr, qn, kvn, wuq, wukv, cos_m, sin_m)


def _attn_kernel(q_ref, k_ref, v_ref, o_ref, *, sub):
    k = k_ref[...]
    v = v_ref[...]
    for r in range(q_ref.shape[0] // sub):
        rows = slice(r * sub, (r + 1) * sub)
        s = _dot_nt(q_ref[rows, :], k)
        p = jnp.exp(s - jnp.max(s, axis=-1, keepdims=True))
        denom = jnp.sum(p, axis=-1, keepdims=True)
        o_ref[rows, :] = (_dot(p.astype(BF16), v) / denom).astype(BF16)


def _attention(q, k, v, batch, seq, heads):
    t = q.shape[0]
    tq, sub = 512, 256
    nq = seq // tq
    return pl.pallas_call(
        functools.partial(_attn_kernel, sub=sub),
        grid=(batch, heads, nq),
        in_specs=[
            pl.BlockSpec((tq, MLA_HEAD_PAD), lambda b, h, i: (b * nq + i, h)),
            pl.BlockSpec((seq, MLA_HEAD_PAD), lambda b, h, i: (b, h)),
            pl.BlockSpec((seq, MLA_V), lambda b, h, i: (b, h)),
        ],
        out_specs=pl.BlockSpec((tq, MLA_V), lambda b, h, i: (b * nq + i, h)),
        out_shape=jax.ShapeDtypeStruct((t, heads * MLA_V), BF16),
        compiler_params=_cparams(("parallel", "parallel", "arbitrary"), 48),
        name="mla_attention",
    )(q, k, v)


def _mla_weights(w_in, w_uq, w_ukv, heads):
    d = w_in.shape[0]
    q_lora = w_uq.shape[0]
    kv_lora = w_ukv.shape[0]
    half = MLA_ROPE // 2
    pad = LANES - MLA_ROPE

    def swap(w):
        return jnp.concatenate([w[..., half:], w[..., :half]], axis=-1)

    def pad_lanes(w):
        return jnp.pad(w, [(0, 0)] * (w.ndim - 1) + [(0, pad)])

    wcq = w_in[:, :q_lora]
    wckv = w_in[:, q_lora:q_lora + kv_lora]
    wkr = w_in[:, q_lora + kv_lora:]
    wkr = jnp.concatenate([pad_lanes(wkr), pad_lanes(swap(wkr))], axis=-1)
    uq = w_uq.reshape(q_lora, heads, MLA_DQK)
    uq_rope = uq[..., MLA_NOPE:]
    wuq = jnp.concatenate([
        uq[..., :MLA_NOPE].reshape(q_lora, heads * MLA_NOPE),
        pad_lanes(uq_rope).reshape(q_lora, heads * LANES),
        pad_lanes(swap(uq_rope)).reshape(q_lora, heads * LANES),
    ], axis=-1)
    ukv = w_ukv.reshape(kv_lora, heads, MLA_NOPE + MLA_V)
    wukv = jnp.concatenate([
        ukv[..., :MLA_NOPE].reshape(kv_lora, heads * MLA_NOPE),
        ukv[..., MLA_NOPE:].reshape(kv_lora, heads * MLA_V),
    ], axis=-1)
    return [w.astype(BF16) for w in (wcq, wckv, wkr, wuq, wukv)]


def kernel(x, c, positions, norm_w, ada_w, ada_b, ret_w_in, ret_decay_logit_fwd,
           ret_decay_logit_bwd, ret_gn_w, ret_w_out, mla_w_in, mla_q_norm, mla_w_uq,
           mla_kv_norm, mla_w_ukv, mla_w_out, mlp_w1, mlp_w2):
    batch, seq, d = x.shape
    depth = norm_w.shape[0]
    assert depth == 2
    t = batch * seq
    ret_heads = ret_decay_logit_fwd.shape[1]
    mla_heads = mla_w_out.shape[1] // MLA_V

    mod = _ada_mod(c, ada_w, ada_b).reshape(depth, batch, 1, N_ADA * d)
    nw = norm_w.reshape(depth, 4, 1, d)
    cos_r, sin_r, cos_m, sin_m = _rope_tables(positions)
    x2 = x.reshape(t, d)

    proj = _ret_proj(x2, nw, mod, 0, ret_w_in[0].astype(BF16), cos_r, sin_r, seq)
    y = _retention(proj, ret_decay_logit_fwd[0], ret_decay_logit_bwd[0], ret_gn_w[0],
                   batch, seq, ret_heads)
    x2, h = _outproj(y, ret_w_out[0].astype(BF16), x2, nw, mod, 0, seq)
    x2, h = _mlp(h, mlp_w1[0].astype(BF16), mlp_w2[0].astype(BF16), x2, nw, mod, 0, seq, True)

    wcq, wckv, wkr, wuq, wukv = _mla_weights(mla_w_in[0], mla_w_uq[0], mla_w_ukv[0], mla_heads)
    q, k, v = _mla_proj(h, wcq, wckv, wkr, mla_q_norm[0], mla_kv_norm[0], wuq, wukv,
                        cos_m, sin_m, mla_heads)
    o = _attention(q, k, v, batch, seq, mla_heads)
    x2, h = _outproj(o, mla_w_out[0].astype(BF16), x2, nw, mod, 1, seq)
    (x2,) = _mlp(h, mlp_w1[1].astype(BF16), mlp_w2[1].astype(BF16), x2, nw, mod, 1, seq, False)
    return x2.reshape(batch, seq, d)
```

```python
import functools

import numpy as np
import jax
import jax.numpy as jnp
from jax import lax
from jax.experimental import pallas as pl
from jax.experimental.pallas import tpu as pltpu

F32 = jnp.float32
BF16 = jnp.bfloat16

RMS_EPS = 1e-6
GN_EPS = 1e-5
ROPE_THETA = 10000.0
N_ADA = 6

RET_DK = 256
RET_DV = 512
RET_CHUNK = 256

MLA_NOPE = 128
MLA_ROPE = 64
MLA_V = 128
MLA_DQK = MLA_NOPE + MLA_ROPE
MLA_HEAD_PAD = 256

LANES = 128
MIB = 1024 * 1024


def _cparams(semantics, vmem_mib):
    return pltpu.CompilerParams(dimension_semantics=semantics,
                                vmem_limit_bytes=vmem_mib * MIB)


def _rms(t, w):
    return t * lax.rsqrt(jnp.mean(t * t, axis=-1, keepdims=True) + RMS_EPS) * w


def _dot(a, b):
    return jnp.dot(a, b, preferred_element_type=F32)


def _dot_nt(a, b):
    return lax.dot_general(a, b, (((1,), (1,)), ((), ())), preferred_element_type=F32)


def _dot_tn(a, b):
    return lax.dot_general(a, b, (((0,), (0,)), ((), ())), preferred_element_type=F32)


def _ada_kernel(c_ref, w_ref, b_ref, o_ref):
    c_act = jax.nn.silu(c_ref[...])
    o_ref[...] = _dot(c_act.astype(BF16), w_ref[...].astype(BF16)) + b_ref[...]


def _ada_mod(c, ada_w, ada_b):
    depth, d, n = ada_w.shape
    b = c.shape[0]
    tn = 1024
    return pl.pallas_call(
        _ada_kernel,
        grid=(depth, n // tn),
        in_specs=[
            pl.BlockSpec((b, d), lambda l, j: (0, 0)),
            pl.BlockSpec((None, d, tn), lambda l, j: (l, 0, j)),
            pl.BlockSpec((None, 1, tn), lambda l, j: (l, 0, j)),
        ],
        out_specs=pl.BlockSpec((None, b, tn), lambda l, j: (l, 0, j)),
        out_shape=jax.ShapeDtypeStruct((depth, b, n), F32),
        compiler_params=_cparams(("arbitrary", "arbitrary"), 32),
        name="ada_mod",
    )(c, ada_w, ada_b.reshape(depth, 1, n))


def _rope_kernel(pos_ref, inv_r_ref, inv_m_ref, sign_m_ref, cr_ref, sr_ref, cm_ref, sm_ref):
    pos = pos_ref[...].astype(F32)
    ang_r = pos * inv_r_ref[...]
    cr_ref[...] = jnp.cos(ang_r)
    sr_ref[...] = jnp.sin(ang_r)
    ang_m = pos * inv_m_ref[...]
    cm_ref[...] = jnp.cos(ang_m)
    sm_ref[...] = jnp.sin(ang_m) * sign_m_ref[...]


def _rope_tables(positions):
    t = positions.size
    ts = 2048
    half_r = RET_DK // 2
    half_m = MLA_ROPE // 2
    inv_r = ROPE_THETA ** (-jnp.arange(0, RET_DK, 2, dtype=F32) / RET_DK)
    inv_m = ROPE_THETA ** (-jnp.arange(0, MLA_ROPE, 2, dtype=F32) / MLA_ROPE)
    reps = LANES // half_m
    inv_m = jnp.tile(inv_m, reps)
    sign_m = jnp.tile(jnp.concatenate([-jnp.ones((half_m,), F32), jnp.ones((half_m,), F32)]), reps // 2)
    tab = jax.ShapeDtypeStruct((t, LANES), F32)
    vec = pl.BlockSpec((1, LANES), lambda i: (0, 0))
    out = pl.BlockSpec((ts, LANES), lambda i: (i, 0))
    assert half_r == LANES
    return pl.pallas_call(
        _rope_kernel,
        grid=(t // ts,),
        in_specs=[pl.BlockSpec((ts, 1), lambda i: (i, 0)), vec, vec, vec],
        out_specs=[out, out, out, out],
        out_shape=[tab, tab, tab, tab],
        compiler_params=_cparams(("arbitrary",), 32),
        name="rope_tables",
    )(positions.reshape(t, 1), inv_r.reshape(1, LANES), inv_m.reshape(1, LANES), sign_m.reshape(1, LANES))


def _ret_proj_kernel(x_ref, nw_ref, sc_ref, sh_ref, w_ref, cos_ref, sin_ref, o_ref, h_scr,
                     *, k_scale):
    j = pl.program_id(1)

    @pl.when(j == 0)
    def _():
        hn = _rms(x_ref[...], nw_ref[...])
        h_scr[...] = (hn * (1.0 + sc_ref[...]) + sh_ref[...]).astype(BF16)

    p = _dot(h_scr[...], w_ref[...])
    tn = p.shape[1]

    @pl.when(j < 2)
    def _():
        scale = jnp.where(j == 1, k_scale, 1.0).astype(F32)
        cos = cos_ref[...] * scale
        sin = sin_ref[...] * scale
        half = RET_DK // 2
        for g in range(tn // RET_DK):
            t1 = p[:, g * RET_DK:g * RET_DK + half]
            t2 = p[:, g * RET_DK + half:(g + 1) * RET_DK]
            o_ref[:, g * RET_DK:g * RET_DK + half] = (t1 * cos - t2 * sin).astype(BF16)
            o_ref[:, g * RET_DK + half:(g + 1) * RET_DK] = (t1 * sin + t2 * cos).astype(BF16)

    @pl.when(j >= 2)
    def _():
        o_ref[...] = p.astype(BF16)


def _ret_proj(x2, nw, mod, layer, w_in, cos_r, sin_r, seq):
    t, d = x2.shape
    n = w_in.shape[1]
    heads_dk = (n // 6)
    tm, tn = 1024, 1024
    assert heads_dk == tn
    per_b = seq // tm
    vec = lambda k: pl.BlockSpec((None, None, 1, d), lambda i, j: (layer, i // per_b, 0, k))
    return pl.pallas_call(
        functools.partial(_ret_proj_kernel, k_scale=RET_DK ** -0.5),
        grid=(t // tm, n // tn),
        in_specs=[
            pl.BlockSpec((tm, d), lambda i, j: (i, 0)),
            pl.BlockSpec((None, None, 1, d), lambda i, j: (layer, 0, 0, 0)),
            vec(1), vec(0),
            pl.BlockSpec((d, tn), lambda i, j: (0, j)),
            pl.BlockSpec((tm, LANES), lambda i, j: (i, 0)),
            pl.BlockSpec((tm, LANES), lambda i, j: (i, 0)),
        ],
        out_specs=pl.BlockSpec((tm, tn), lambda i, j: (i, j)),
        out_shape=jax.ShapeDtypeStruct((t, n), BF16),
        scratch_shapes=[pltpu.VMEM((tm, d), BF16)],
        compiler_params=_cparams(("parallel", "arbitrary"), 48),
        name="ret_proj",
    )(x2, nw, mod, mod, w_in, cos_r, sin_r)


def _retention_kernel(lf_ref, lb_ref, q_ref, k_ref, v_ref, g_ref, gn_ref, o_ref,
                      y_scr, sf_scr, sb_scr):
    c = RET_CHUNK
    n = q_ref.shape[0] // c
    lg_f = jax.nn.log_sigmoid(lf_ref[...].astype(F32))
    lg_b = jax.nn.log_sigmoid(lb_ref[...].astype(F32))
    row = lax.broadcasted_iota(jnp.int32, (c, c), 0).astype(F32)
    col = lax.broadcasted_iota(jnp.int32, (c, c), 1).astype(F32)
    pos = lax.broadcasted_iota(jnp.int32, (c, 1), 0).astype(F32)
    diff = row - col
    mask_f = diff >= 0
    mask_b = diff < 0
    decay_f = jnp.where(mask_f, jnp.exp(jnp.where(mask_f, diff * lg_f, 0.0)), 0.0)
    decay_b = jnp.where(mask_b, jnp.exp(jnp.where(mask_b, -diff * lg_b, 0.0)), 0.0)
    cross_f = jnp.exp((pos + 1.0) * lg_f)
    cross_b = jnp.exp((c - pos) * lg_b)
    sdec_f = jnp.exp((c - 1.0 - pos) * lg_f)
    sdec_b = jnp.exp(pos * lg_b)
    cdec_f = jnp.exp(c * lg_f)
    cdec_b = jnp.exp(c * lg_b)
    gn_w = gn_ref[...]

    sf_scr[...] = jnp.zeros_like(sf_scr)
    sb_scr[...] = jnp.zeros_like(sb_scr)

    def rows(ci):
        return pl.ds(pl.multiple_of(ci * c, c), c)

    def step(ci, st_ref, decay, cross, sdec, cdec):
        r = rows(ci)
        qi, ki, vi = q_ref[r, :], k_ref[r, :], v_ref[r, :]
        s = _dot_nt(qi, ki) * decay
        out = _dot(s.astype(BF16), vi)
        st = st_ref[...]
        out = out + _dot(qi, st.astype(BF16)) * cross
        kd = (ki.astype(F32) * sdec).astype(BF16)
        st_ref[...] = st * cdec + _dot_tn(kd, vi)
        return out

    def finalize(ci, y):
        r = rows(ci)
        mu = jnp.mean(y, axis=-1, keepdims=True)
        yc = y - mu
        var = jnp.mean(yc * yc, axis=-1, keepdims=True)
        yn = yc * lax.rsqrt(var + GN_EPS) * gn_w
        o_ref[r, :] = (jax.nn.silu(g_ref[r, :].astype(F32)) * yn).astype(BF16)

    def first_half(i, carry):
        cb = n - 1 - i
        y_scr[rows(i), :] = step(i, sf_scr, decay_f, cross_f, sdec_f, cdec_f)
        y_scr[rows(cb), :] = step(cb, sb_scr, decay_b, cross_b, sdec_b, cdec_b)
        return carry

    def second_half(i, carry):
        cb = n - 1 - i
        finalize(i, y_scr[rows(i), :] + step(i, sf_scr, decay_f, cross_f, sdec_f, cdec_f))
        finalize(cb, y_scr[rows(cb), :] + step(cb, sb_scr, decay_b, cross_b, sdec_b, cdec_b))
        return carry

    lax.fori_loop(0, n // 2, first_half, 0, unroll=2)
    lax.fori_loop(n // 2, n, second_half, 0, unroll=2)


def _retention(proj, logit_f, logit_b, gn_w, batch, seq, heads):
    t = proj.shape[0]
    hk = heads * RET_DK
    hv = heads * RET_DV
    assert (seq // RET_CHUNK) % 4 == 0
    kb, vb, gb = hk // RET_DK, 2 * hk // RET_DV, (2 * hk + hv) // RET_DV
    scalar = pl.BlockSpec((None, 1, 1), lambda b, h: (h, 0, 0))
    return pl.pallas_call(
        _retention_kernel,
        grid=(batch, heads),
        in_specs=[
            scalar, scalar,
            pl.BlockSpec((seq, RET_DK), lambda b, h: (b, h)),
            pl.BlockSpec((seq, RET_DK), lambda b, h: (b, kb + h)),
            pl.BlockSpec((seq, RET_DV), lambda b, h: (b, vb + h)),
            pl.BlockSpec((seq, RET_DV), lambda b, h: (b, gb + h)),
            pl.BlockSpec((1, RET_DV), lambda b, h: (0, h)),
        ],
        out_specs=pl.BlockSpec((seq, RET_DV), lambda b, h: (b, h)),
        out_shape=jax.ShapeDtypeStruct((t, hv), BF16),
        scratch_shapes=[
            pltpu.VMEM((seq, RET_DV), F32),
            pltpu.VMEM((RET_DK, RET_DV), F32),
            pltpu.VMEM((RET_DK, RET_DV), F32),
        ],
        compiler_params=_cparams(("parallel", "parallel"), 48),
        name="retention",
    )(logit_f.reshape(heads, 1, 1), logit_b.reshape(heads, 1, 1), proj, proj, proj, proj,
      gn_w.reshape(1, hv))


def _outproj_kernel(a_ref, w_ref, x_ref, gate_ref, nw_post_ref, nw_pre_ref, sc_ref, sh_ref,
                    xo_ref, h_ref):
    y = _dot(a_ref[...], w_ref[...])
    x = x_ref[...] + gate_ref[...] * _rms(y, nw_post_ref[...])
    xo_ref[...] = x
    h_ref[...] = (_rms(x, nw_pre_ref[...]) * (1.0 + sc_ref[...]) + sh_ref[...]).astype(BF16)


def _outproj(a, w_out, x2, nw, mod, layer, seq):
    t, d = x2.shape
    kd = a.shape[1]
    tm = 512
    per_b = seq // tm
    vec = lambda k: pl.BlockSpec((None, None, 1, d), lambda i: (layer, i // per_b, 0, k))
    nvec = lambda k: pl.BlockSpec((None, None, 1, d), lambda i: (layer, k, 0, 0))
    row = pl.BlockSpec((tm, d), lambda i: (i, 0))
    return pl.pallas_call(
        _outproj_kernel,
        grid=(t // tm,),
        in_specs=[
            pl.BlockSpec((tm, kd), lambda i: (i, 0)),
            pl.BlockSpec((kd, d), lambda i: (0, 0)),
            row, vec(2), nvec(1), nvec(2), vec(4), vec(3),
        ],
        out_specs=[row, row],
        out_shape=[jax.ShapeDtypeStruct((t, d), F32), jax.ShapeDtypeStruct((t, d), BF16)],
        compiler_params=_cparams(("parallel",), 48),
        name="mixer_outproj",
    )(a, w_out, x2, mod, nw, nw, mod, mod)


def _mlp_kernel(*refs, emit_next):
    if emit_next:
        (h_ref, w1_ref, w2_ref, x_ref, gate_ref, nw_post_ref, nw_pre_ref, sc_ref, sh_ref,
         xo_ref, hn_ref, acc_scr) = refs
    else:
        h_ref, w1_ref, w2_ref, x_ref, gate_ref, nw_post_ref, xo_ref, acc_scr = refs
    j = pl.program_id(1)

    @pl.when(j == 0)
    def _():
        acc_scr[...] = jnp.zeros_like(acc_scr)

    u = jnp.square(jnp.maximum(_dot(h_ref[...], w1_ref[...]), 0.0))
    acc_scr[...] += _dot(u.astype(BF16), w2_ref[...])

    @pl.when(j == pl.num_programs(1) - 1)
    def _():
        x = x_ref[...] + gate_ref[...] * _rms(acc_scr[...], nw_post_ref[...])
        xo_ref[...] = x
        if emit_next:
            hn = _rms(x, nw_pre_ref[...]) * (1.0 + sc_ref[...]) + sh_ref[...]
            hn_ref[...] = hn.astype(BF16)


def _mlp(h, w1, w2, x2, nw, mod, layer, seq, emit_next):
    t, d = x2.shape
    dff = w1.shape[1]
    tm, tf = 512, 1024
    per_b = seq // tm
    vec = lambda l, k: pl.BlockSpec((None, None, 1, d), lambda i, j: (l, i // per_b, 0, k))
    nvec = lambda l, k: pl.BlockSpec((None, None, 1, d), lambda i, j: (l, k, 0, 0))
    row = pl.BlockSpec((tm, d), lambda i, j: (i, 0))
    in_specs = [
        row,
        pl.BlockSpec((d, tf), lambda i, j: (0, j)),
        pl.BlockSpec((tf, d), lambda i, j: (j, 0)),
        row, vec(layer, 5), nvec(layer, 3),
    ]
    args = [h, w1, w2, x2, mod, nw]
    out_specs = [row]
    out_shape = [jax.ShapeDtypeStruct((t, d), F32)]
    if emit_next:
        in_specs += [nvec(layer + 1, 0), vec(layer + 1, 1), vec(layer + 1, 0)]
        args += [nw, mod, mod]
        out_specs.append(row)
        out_shape.append(jax.ShapeDtypeStruct((t, d), BF16))
    return pl.pallas_call(
        functools.partial(_mlp_kernel, emit_next=emit_next),
        grid=(t // tm, dff // tf),
        in_specs=in_specs,
        out_specs=out_specs,
        out_shape=out_shape,
        scratch_shapes=[pltpu.VMEM((tm, d), F32)],
        compiler_params=_cparams(("parallel", "arbitrary"), 48),
        name="mlp_next" if emit_next else "mlp_last",
    )(*args)


def _mla_proj_kernel(h_ref, wcq_ref, wckv_ref, wkr_ref, qn_ref, kvn_ref, wuq_ref, wukv_ref,
                     cos_ref, sin_ref, q_ref, k_ref, v_ref, *, heads, scale):
    h = h_ref[...]
    cos = cos_ref[...]
    sin = sin_ref[...]
    cq = _rms(_dot(h, wcq_ref[...]), qn_ref[...]).astype(BF16)
    ckv = _rms(_dot(h, wckv_ref[...]), kvn_ref[...]).astype(BF16)
    kr = _dot(h, wkr_ref[...])
    kr = (kr[:, :LANES] * cos + kr[:, LANES:] * sin).astype(BF16)
    nope_w = heads * MLA_NOPE
    qcos = cos * scale
    qsin = sin * scale
    for hd in range(heads):
        lo = hd * MLA_HEAD_PAD
        w_n = wuq_ref[:, hd * LANES:(hd + 1) * LANES]
        w_r = wuq_ref[:, nope_w + hd * LANES:nope_w + (hd + 1) * LANES]
        w_s = wuq_ref[:, 2 * nope_w + hd * LANES:2 * nope_w + (hd + 1) * LANES]
        q_ref[:, lo:lo + LANES] = (_dot(cq, w_n) * scale).astype(BF16)
        q_ref[:, lo + LANES:lo + 2 * LANES] = (_dot(cq, w_r) * qcos + _dot(cq, w_s) * qsin).astype(BF16)
        k_ref[:, lo:lo + LANES] = _dot(ckv, wukv_ref[:, hd * LANES:(hd + 1) * LANES]).astype(BF16)
        k_ref[:, lo + LANES:lo + 2 * LANES] = kr
    v_ref[...] = _dot(ckv, wukv_ref[:, nope_w:]).astype(BF16)


def _mla_proj(h, wcq, wckv, wkr, q_norm, kv_norm, wuq, wukv, cos_m, sin_m, heads):
    t, d = h.shape
    tm = 512
    full = lambda a: pl.BlockSpec(a.shape, lambda i: (0,) * a.ndim)
    qn = q_norm.reshape(1, -1)
    kvn = kv_norm.reshape(1, -1)
    row = lambda w: pl.BlockSpec((tm, w), lambda i: (i, 0))
    return pl.pallas_call(
        functools.partial(_mla_proj_kernel, heads=heads, scale=float(MLA_DQK ** -0.5 * np.log2(np.e))),
        grid=(t // tm,),
        in_specs=[row(d), full(wcq), full(wckv), full(wkr), full(qn), full(kvn), full(wuq),
                  full(wukv), row(LANES), row(LANES)],
        out_specs=[row(heads * MLA_HEAD_PAD), row(heads * MLA_HEAD_PAD), row(heads * MLA_V)],
        out_shape=[jax.ShapeDtypeStruct((t, heads * MLA_HEAD_PAD), BF16),
                   jax.ShapeDtypeStruct((t, heads * MLA_HEAD_PAD), BF16),
                   jax.ShapeDtypeStruct((t, heads * MLA_V), BF16)],
        compiler_params=_cparams(("parallel",), 48),
        name="mla_proj",
    )(h, wcq, wckv, wkr, qn, kvn, wuq, wukv, cos_m, sin_m)


def _attn_kernel(q_ref, k_ref, v_ref, o_ref, v1_scr, *, sub):
    v1_scr[:, :MLA_V] = v_ref[...]
    v1_scr[:, MLA_V:] = jnp.ones((v_ref.shape[0], v1_scr.shape[1] - MLA_V), BF16)
    k = k_ref[...]
    for r in range(q_ref.shape[0] // sub):
        rows = slice(r * sub, (r + 1) * sub)
        s = _dot_nt(q_ref[rows, :], k)
        p = jnp.exp2(s - jnp.max(s, axis=-1, keepdims=True))
        ov = _dot(p.astype(BF16), v1_scr[...])
        o_ref[rows, :] = (ov[:, :MLA_V] / ov[:, MLA_V:MLA_V + 1]).astype(BF16)


def _attention(q, k, v, batch, seq, heads):
    t = q.shape[0]
    sub = 256
    return pl.pallas_call(
        functools.partial(_attn_kernel, sub=sub),
        grid=(batch, heads),
        in_specs=[
            pl.BlockSpec((seq, MLA_HEAD_PAD), lambda b, h: (b, h)),
            pl.BlockSpec((seq, MLA_HEAD_PAD), lambda b, h: (b, h)),
            pl.BlockSpec((seq, MLA_V), lambda b, h: (b, h)),
        ],
        out_specs=pl.BlockSpec((seq, MLA_V), lambda b, h: (b, h)),
        out_shape=jax.ShapeDtypeStruct((t, heads * MLA_V), BF16),
        scratch_shapes=[pltpu.VMEM((seq, 2 * MLA_V), BF16)],
        compiler_params=_cparams(("parallel", "parallel"), 48),
        name="mla_attention",
    )(q, k, v)


def _mla_weights(w_in, w_uq, w_ukv, heads):
    d = w_in.shape[0]
    q_lora = w_uq.shape[0]
    kv_lora = w_ukv.shape[0]
    half = MLA_ROPE // 2
    pad = LANES - MLA_ROPE

    def swap(w):
        return jnp.concatenate([w[..., half:], w[..., :half]], axis=-1)

    def pad_lanes(w):
        return jnp.pad(w, [(0, 0)] * (w.ndim - 1) + [(0, pad)])

    wcq = w_in[:, :q_lora]
    wckv = w_in[:, q_lora:q_lora + kv_lora]
    wkr = w_in[:, q_lora + kv_lora:]
    wkr = jnp.concatenate([pad_lanes(wkr), pad_lanes(swap(wkr))], axis=-1)
    uq = w_uq.reshape(q_lora, heads, MLA_DQK)
    uq_rope = uq[..., MLA_NOPE:]
    wuq = jnp.concatenate([
        uq[..., :MLA_NOPE].reshape(q_lora, heads * MLA_NOPE),
        pad_lanes(uq_rope).reshape(q_lora, heads * LANES),
        pad_lanes(swap(uq_rope)).reshape(q_lora, heads * LANES),
    ], axis=-1)
    ukv = w_ukv.reshape(kv_lora, heads, MLA_NOPE + MLA_V)
    wukv = jnp.concatenate([
        ukv[..., :MLA_NOPE].reshape(kv_lora, heads * MLA_NOPE),
        ukv[..., MLA_NOPE:].reshape(kv_lora, heads * MLA_V),
    ], axis=-1)
    return [w.astype(BF16) for w in (wcq, wckv, wkr, wuq, wukv)]


def kernel(x, c, positions, norm_w, ada_w, ada_b, ret_w_in, ret_decay_logit_fwd,
           ret_decay_logit_bwd, ret_gn_w, ret_w_out, mla_w_in, mla_q_norm, mla_w_uq,
           mla_kv_norm, mla_w_ukv, mla_w_out, mlp_w1, mlp_w2):
    batch, seq, d = x.shape
    depth = norm_w.shape[0]
    assert depth == 2
    t = batch * seq
    ret_heads = ret_decay_logit_fwd.shape[1]
    mla_heads = mla_w_out.shape[1] // MLA_V

    mod = _ada_mod(c, ada_w, ada_b).reshape(depth, batch, 1, N_ADA * d)
    nw = norm_w.reshape(depth, 4, 1, d)
    cos_r, sin_r, cos_m, sin_m = _rope_tables(positions)
    x2 = x.reshape(t, d)

    proj = _ret_proj(x2, nw, mod, 0, ret_w_in[0].astype(BF16), cos_r, sin_r, seq)
    y = _retention(proj, ret_decay_logit_fwd[0], ret_decay_logit_bwd[0], ret_gn_w[0],
                   batch, seq, ret_heads)
    x2, h = _outproj(y, ret_w_out[0].astype(BF16), x2, nw, mod, 0, seq)
    x2, h = _mlp(h, mlp_w1[0].astype(BF16), mlp_w2[0].astype(BF16), x2, nw, mod, 0, seq, True)

    wcq, wckv, wkr, wuq, wukv = _mla_weights(mla_w_in[0], mla_w_uq[0], mla_w_ukv[0], mla_heads)
    q, k, v = _mla_proj(h, wcq, wckv, wkr, mla_q_norm[0], mla_kv_norm[0], wuq, wukv,
                        cos_m, sin_m, mla_heads)
    o = _attention(q, k, v, batch, seq, mla_heads)
    x2, h = _outproj(o, mla_w_out[0].astype(BF16), x2, nw, mod, 1, seq)
    (x2,) = _mlp(h, mlp_w1[1].astype(BF16), mlp_w2[1].astype(BF16), x2, nw, mod, 1, seq, False)
    return x2.reshape(batch, seq, d)
```

```python
import functools

import numpy as np
import jax
import jax.numpy as jnp
from jax import lax
from jax.experimental import pallas as pl
from jax.experimental.pallas import tpu as pltpu

F32 = jnp.float32
BF16 = jnp.bfloat16

RMS_EPS = 1e-6
GN_EPS = 1e-5
ROPE_THETA = 10000.0
N_ADA = 6

RET_DK = 256
RET_DV = 512
RET_CHUNK = 256

MLA_NOPE = 128
MLA_ROPE = 64
MLA_V = 128
MLA_DQK = MLA_NOPE + MLA_ROPE
MLA_HEAD_PAD = 256

LANES = 128
MIB = 1024 * 1024

MLP_TM, MLP_SPLITS, MLP_TF, MLP_VMEM_MIB = 512, 2, 1024, 56


def _cparams(semantics, vmem_mib):
    return pltpu.CompilerParams(dimension_semantics=semantics,
                                vmem_limit_bytes=vmem_mib * MIB)


def _rms(t, w):
    return t * lax.rsqrt(jnp.mean(t * t, axis=-1, keepdims=True) + RMS_EPS) * w


def _dot(a, b):
    return jnp.dot(a, b, preferred_element_type=F32)


def _dot_nt(a, b):
    return lax.dot_general(a, b, (((1,), (1,)), ((), ())), preferred_element_type=F32)


def _dot_tn(a, b):
    return lax.dot_general(a, b, (((0,), (0,)), ((), ())), preferred_element_type=F32)


def _ada_kernel(c_ref, w_ref, b_ref, o_ref):
    c_act = jax.nn.silu(c_ref[...])
    o_ref[...] = _dot(c_act.astype(BF16), w_ref[...].astype(BF16)) + b_ref[...]


def _ada_mod(c, ada_w, ada_b):
    depth, d, n = ada_w.shape
    b = c.shape[0]
    tn = 1024
    return pl.pallas_call(
        _ada_kernel,
        grid=(depth, n // tn),
        in_specs=[
            pl.BlockSpec((b, d), lambda l, j: (0, 0)),
            pl.BlockSpec((None, d, tn), lambda l, j: (l, 0, j)),
            pl.BlockSpec((None, 1, tn), lambda l, j: (l, 0, j)),
        ],
        out_specs=pl.BlockSpec((None, b, tn), lambda l, j: (l, 0, j)),
        out_shape=jax.ShapeDtypeStruct((depth, b, n), F32),
        compiler_params=_cparams(("arbitrary", "arbitrary"), 32),
        name="ada_mod",
    )(c, ada_w, ada_b.reshape(depth, 1, n))


def _rope_freqs():
    half_m = MLA_ROPE // 2
    assert RET_DK // 2 == LANES
    inv_r = ROPE_THETA ** (-jnp.arange(0, RET_DK, 2, dtype=F32) / RET_DK)
    inv_m = ROPE_THETA ** (-jnp.arange(0, MLA_ROPE, 2, dtype=F32) / MLA_ROPE)
    inv_m = jnp.tile(inv_m, LANES // half_m)
    sign_m = jnp.concatenate([-jnp.ones((half_m,), F32), jnp.ones((half_m,), F32),
                              jnp.zeros((LANES - MLA_ROPE,), F32)])
    return inv_r.reshape(1, LANES), inv_m.reshape(1, LANES), sign_m.reshape(1, LANES)


def _ret_proj_kernel(x_ref, nw_ref, sc_ref, sh_ref, w_ref, pos_ref, inv_ref, o_ref,
                     *, hk, k_scale, splits, tn):
    rows_per = x_ref.shape[0] // splits
    half = RET_DK // 2
    for r in range(splits):
        rows = slice(r * rows_per, (r + 1) * rows_per)
        hn = _rms(x_ref[rows, :], nw_ref[...])
        h = (hn * (1.0 + sc_ref[...]) + sh_ref[...]).astype(BF16)
        ang = pos_ref[rows, :].astype(F32) * inv_ref[...]
        cos = jnp.cos(ang)
        sin = jnp.sin(ang)
        n_chunks = w_ref.shape[1] // tn
        rope_chunks = 2 * hk // tn
        for j in list(range(rope_chunks, n_chunks)) + list(range(rope_chunks)):
            p = _dot(h, w_ref[:, j * tn:(j + 1) * tn])
            if j * tn >= 2 * hk:
                o_ref[rows, j * tn:(j + 1) * tn] = p.astype(BF16)
                continue
            scale = k_scale if j * tn >= hk else 1.0
            for g in range(tn // RET_DK):
                lo = j * tn + g * RET_DK
                t1 = p[:, g * RET_DK:g * RET_DK + half] * scale
                t2 = p[:, g * RET_DK + half:(g + 1) * RET_DK] * scale
                o_ref[rows, lo:lo + half] = (t1 * cos - t2 * sin).astype(BF16)
                o_ref[rows, lo + half:lo + RET_DK] = (t1 * sin + t2 * cos).astype(BF16)


def _ret_proj(x2, nw, mod, layer, w_in, pos, inv_r, seq, heads):
    t, d = x2.shape
    n = w_in.shape[1]
    hk = heads * RET_DK
    tm, splits, tn = 512, 2, 1024
    assert hk % tn == 0
    per_b = seq // tm
    vec = lambda k: pl.BlockSpec((None, None, 1, d), lambda i: (layer, i // per_b, 0, k))
    return pl.pallas_call(
        functools.partial(_ret_proj_kernel, hk=hk, k_scale=RET_DK ** -0.5, splits=splits, tn=tn),
        grid=(t // tm,),
        in_specs=[
            pl.BlockSpec((tm, d), lambda i: (i, 0)),
            pl.BlockSpec((None, None, 1, d), lambda i: (layer, 0, 0, 0)),
            vec(1), vec(0),
            _resident((d, n)),
            pl.BlockSpec((tm, 1), lambda i: (i, 0)),
            _resident((1, LANES)),
        ],
        out_specs=pl.BlockSpec((tm, n), lambda i: (i, 0)),
        out_shape=jax.ShapeDtypeStruct((t, n), BF16),
        compiler_params=_cparams(("parallel",), 48),
        name="ret_proj",
    )(x2, nw, mod, mod, w_in, pos, inv_r)


def _retention_kernel(lf_ref, lb_ref, q_ref, k_ref, v_ref, g_ref, gn_ref, o_ref,
                      y_scr, sf_scr, sb_scr):
    c = RET_CHUNK
    n = q_ref.shape[0] // c
    lg_f = jax.nn.log_sigmoid(lf_ref[...].astype(F32))
    lg_b = jax.nn.log_sigmoid(lb_ref[...].astype(F32))
    row = lax.broadcasted_iota(jnp.int32, (c, c), 0).astype(F32)
    col = lax.broadcasted_iota(jnp.int32, (c, c), 1).astype(F32)
    pos = lax.broadcasted_iota(jnp.int32, (c, 1), 0).astype(F32)
    diff = row - col
    mask_f = diff >= 0
    mask_b = diff < 0
    decay_f = jnp.where(mask_f, jnp.exp(jnp.where(mask_f, diff * lg_f, 0.0)), 0.0)
    decay_b = jnp.where(mask_b, jnp.exp(jnp.where(mask_b, -diff * lg_b, 0.0)), 0.0)
    cross_f = jnp.exp((pos + 1.0) * lg_f)
    cross_b = jnp.exp((c - pos) * lg_b)
    sdec_f = jnp.exp((c - 1.0 - pos) * lg_f)
    sdec_b = jnp.exp(pos * lg_b)
    cdec_f = jnp.exp(c * lg_f)
    cdec_b = jnp.exp(c * lg_b)
    gn_w = gn_ref[...]

    sf_scr[...] = jnp.zeros_like(sf_scr)
    sb_scr[...] = jnp.zeros_like(sb_scr)

    def rows(ci):
        return pl.ds(pl.multiple_of(ci * c, c), c)

    def step(ci, st_ref, decay, cross, sdec, cdec):
        r = rows(ci)
        qi, ki, vi = q_ref[r, :], k_ref[r, :], v_ref[r, :]
        s = _dot_nt(qi, ki) * decay
        out = _dot(s.astype(BF16), vi)
        st = st_ref[...]
        out = out + _dot(qi, st.astype(BF16)) * cross
        kd = (ki.astype(F32) * sdec).astype(BF16)
        st_ref[...] = st * cdec + _dot_tn(kd, vi)
        return out

    def finalize(ci, y):
        r = rows(ci)
        mu = jnp.mean(y, axis=-1, keepdims=True)
        yc = y - mu
        var = jnp.mean(yc * yc, axis=-1, keepdims=True)
        yn = yc * lax.rsqrt(var + GN_EPS) * gn_w
        o_ref[r, :] = (jax.nn.silu(g_ref[r, :].astype(F32)) * yn).astype(BF16)

    def first_half(i, carry):
        cb = n - 1 - i
        y_scr[rows(i), :] = step(i, sf_scr, decay_f, cross_f, sdec_f, cdec_f)
        y_scr[rows(cb), :] = step(cb, sb_scr, decay_b, cross_b, sdec_b, cdec_b)
        return carry

    def second_half(i, carry):
        cb = n - 1 - i
        finalize(i, y_scr[rows(i), :] + step(i, sf_scr, decay_f, cross_f, sdec_f, cdec_f))
        finalize(cb, y_scr[rows(cb), :] + step(cb, sb_scr, decay_b, cross_b, sdec_b, cdec_b))
        return carry

    lax.fori_loop(0, n // 2, first_half, 0, unroll=2)
    lax.fori_loop(n // 2, n, second_half, 0, unroll=2)


def _retention(proj, logit_f, logit_b, gn_w, batch, seq, heads):
    t = proj.shape[0]
    hk = heads * RET_DK
    hv = heads * RET_DV
    assert (seq // RET_CHUNK) % 4 == 0
    kb, vb, gb = hk // RET_DK, 2 * hk // RET_DV, (2 * hk + hv) // RET_DV
    scalar = pl.BlockSpec((None, 1, 1), lambda b, h: (h, 0, 0))
    return pl.pallas_call(
        _retention_kernel,
        grid=(batch, heads),
        in_specs=[
            scalar, scalar,
            pl.BlockSpec((seq, RET_DK), lambda b, h: (b, h)),
            pl.BlockSpec((seq, RET_DK), lambda b, h: (b, kb + h)),
            pl.BlockSpec((seq, RET_DV), lambda b, h: (b, vb + h)),
            pl.BlockSpec((seq, RET_DV), lambda b, h: (b, gb + h)),
            pl.BlockSpec((1, RET_DV), lambda b, h: (0, h)),
        ],
        out_specs=pl.BlockSpec((seq, RET_DV), lambda b, h: (b, h)),
        out_shape=jax.ShapeDtypeStruct((t, hv), BF16),
        scratch_shapes=[
            pltpu.VMEM((seq, RET_DV), F32),
            pltpu.VMEM((RET_DK, RET_DV), F32),
            pltpu.VMEM((RET_DK, RET_DV), F32),
        ],
        compiler_params=_cparams(("parallel", "parallel"), 48),
        name="retention",
    )(logit_f.reshape(heads, 1, 1), logit_b.reshape(heads, 1, 1), proj, proj, proj, proj,
      gn_w.reshape(1, hv))


def _mixer_mlp_kernel(*refs, emit_next, splits, tf):
    if emit_next:
        (a_ref, wo_ref, x_ref, ga_ref, nw1_ref, nw2_ref, scm_ref, shm_ref, w1_ref, w2_ref,
         gm_ref, nw3_ref, nwn_ref, scn_ref, shn_ref, xo_ref, hn_ref) = refs
    else:
        (a_ref, wo_ref, x_ref, ga_ref, nw1_ref, nw2_ref, scm_ref, shm_ref, w1_ref, w2_ref,
         gm_ref, nw3_ref, xo_ref) = refs
    rows_per = x_ref.shape[0] // splits
    dff = w1_ref.shape[1]
    for r in range(splits):
        rows = slice(r * rows_per, (r + 1) * rows_per)
        y = _dot(a_ref[rows, :], wo_ref[...])
        x1 = x_ref[rows, :] + ga_ref[...] * _rms(y, nw1_ref[...])
        h = (_rms(x1, nw2_ref[...]) * (1.0 + scm_ref[...]) + shm_ref[...]).astype(BF16)
        acc = None
        for j in range(dff // tf):
            cols = slice(j * tf, (j + 1) * tf)
            u = jnp.square(jnp.maximum(_dot(h, w1_ref[:, cols]), 0.0)).astype(BF16)
            part = _dot(u, w2_ref[cols, :])
            acc = part if acc is None else acc + part
        x2 = x1 + gm_ref[...] * _rms(acc, nw3_ref[...])
        xo_ref[rows, :] = x2
        if emit_next:
            hn = _rms(x2, nwn_ref[...]) * (1.0 + scn_ref[...]) + shn_ref[...]
            hn_ref[rows, :] = hn.astype(BF16)


def _resident(shape):
    return pl.BlockSpec(shape, lambda *_: (0,) * len(shape), pipeline_mode=pl.Buffered(1))


def _mixer_mlp(a, w_out, x2, w1, w2, nw, mod, layer, seq, emit_next):
    t, d = x2.shape
    kd = a.shape[1]
    dff = w1.shape[1]
    tm, splits, tf = MLP_TM, MLP_SPLITS, MLP_TF
    per_b = seq // tm
    vec = lambda l, k: pl.BlockSpec((None, None, 1, d), lambda i: (l, i // per_b, 0, k))
    nvec = lambda l, k: pl.BlockSpec((None, None, 1, d), lambda i: (l, k, 0, 0))
    row = pl.BlockSpec((tm, d), lambda i: (i, 0))
    in_specs = [
        pl.BlockSpec((tm, kd), lambda i: (i, 0)), _resident((kd, d)), row,
        vec(layer, 2), nvec(layer, 1), nvec(layer, 2), vec(layer, 4), vec(layer, 3),
        _resident((d, dff)), _resident((dff, d)), vec(layer, 5), nvec(layer, 3),
    ]
    args = [a, w_out, x2, mod, nw, nw, mod, mod, w1, w2, mod, nw]
    out_specs = [row]
    out_shape = [jax.ShapeDtypeStruct((t, d), F32)]
    if emit_next:
        in_specs += [nvec(layer + 1, 0), vec(layer + 1, 1), vec(layer + 1, 0)]
        args += [nw, mod, mod]
        out_specs.append(row)
        out_shape.append(jax.ShapeDtypeStruct((t, d), BF16))
    return pl.pallas_call(
        functools.partial(_mixer_mlp_kernel, emit_next=emit_next, splits=splits, tf=tf),
        grid=(t // tm,),
        in_specs=in_specs,
        out_specs=out_specs,
        out_shape=out_shape,
        compiler_params=_cparams(("parallel",), MLP_VMEM_MIB),
        name="mixer_mlp_next" if emit_next else "mixer_mlp_last",
    )(*args)


def _mla_proj_kernel(h_ref, wc_ref, qn_ref, kvn_ref, wuq_ref, wukv_ref, pos_ref, inv_ref,
                     sign_ref, q_ref, k_ref, v_ref, *, heads, scale, splits):
    q_lora = qn_ref.shape[1]
    kv_lora = kvn_ref.shape[1]
    nope_w = heads * MLA_NOPE
    rows_per = h_ref.shape[0] // splits
    first_half = lax.broadcasted_iota(jnp.int32, (1, LANES), 1) < MLA_ROPE // 2

    for r in range(splits):
        rows = slice(r * rows_per, (r + 1) * rows_per)
        ang = pos_ref[rows, :].astype(F32) * inv_ref[...]
        cos = jnp.cos(ang)
        sin = jnp.sin(ang) * sign_ref[...]

        def rotary(t, cos=cos, sin=sin):
            partner = jnp.where(first_half, pltpu.roll(t, LANES - MLA_ROPE // 2, 1),
                                pltpu.roll(t, MLA_ROPE // 2, 1))
            return t * cos + partner * sin

        c_all = _dot(h_ref[rows, :], wc_ref[...])
        cq = _rms(c_all[:, :q_lora], qn_ref[...]).astype(BF16)
        ckv = _rms(c_all[:, q_lora:q_lora + kv_lora], kvn_ref[...]).astype(BF16)
        kr = rotary(c_all[:, q_lora + kv_lora:]).astype(BF16)
        kva = _dot(ckv, wukv_ref[...])
        v_ref[rows, :] = kva[:, nope_w:].astype(BF16)
        for hd in range(heads):
            lo = hd * MLA_HEAD_PAD
            qh = _dot(cq, wuq_ref[:, lo:lo + MLA_HEAD_PAD]) * scale
            q_ref[rows, lo:lo + LANES] = qh[:, :LANES].astype(BF16)
            q_ref[rows, lo + LANES:lo + 2 * LANES] = rotary(qh[:, LANES:]).astype(BF16)
            k_ref[rows, lo:lo + LANES] = kva[:, hd * LANES:(hd + 1) * LANES].astype(BF16)
            k_ref[rows, lo + LANES:lo + 2 * LANES] = kr


def _mla_proj(h, wc, q_norm, kv_norm, wuq, wukv, pos, inv_m, sign_m, heads):
    t, d = h.shape
    tm, splits = 512, 2
    qn = q_norm.reshape(1, -1)
    kvn = kv_norm.reshape(1, -1)
    row = lambda w: pl.BlockSpec((tm, w), lambda i: (i, 0))
    return pl.pallas_call(
        functools.partial(_mla_proj_kernel, heads=heads, splits=splits,
                          scale=float(MLA_DQK ** -0.5 * np.log2(np.e))),
        grid=(t // tm,),
        in_specs=[row(d), _resident(wc.shape), _resident(qn.shape), _resident(kvn.shape),
                  _resident(wuq.shape), _resident(wukv.shape), row(1), _resident((1, LANES)),
                  _resident((1, LANES))],
        out_specs=[row(heads * MLA_HEAD_PAD), row(heads * MLA_HEAD_PAD), row(heads * MLA_V)],
        out_shape=[jax.ShapeDtypeStruct((t, heads * MLA_HEAD_PAD), BF16),
                   jax.ShapeDtypeStruct((t, heads * MLA_HEAD_PAD), BF16),
                   jax.ShapeDtypeStruct((t, heads * MLA_V), BF16)],
        compiler_params=_cparams(("parallel",), 48),
        name="mla_proj",
    )(h, wc, qn, kvn, wuq, wukv, pos, inv_m, sign_m)


def _attn_kernel(q_ref, k_ref, v_ref, o_ref, v1_scr, *, sub):
    v1_scr[:, :MLA_V] = v_ref[...]
    v1_scr[:, MLA_V:] = jnp.ones((v_ref.shape[0], v1_scr.shape[1] - MLA_V), BF16)
    k = k_ref[...]
    for r in range(q_ref.shape[0] // sub):
        rows = slice(r * sub, (r + 1) * sub)
        s = _dot_nt(q_ref[rows, :], k)
        p = jnp.exp2(s - jnp.max(s, axis=-1, keepdims=True))
        ov = _dot(p.astype(BF16), v1_scr[...])
        o_ref[rows, :] = (ov[:, :MLA_V] / ov[:, MLA_V:MLA_V + 1]).astype(BF16)


def _attention(q, k, v, batch, seq, heads):
    t = q.shape[0]
    sub = 256
    return pl.pallas_call(
        functools.partial(_attn_kernel, sub=sub),
        grid=(batch, heads),
        in_specs=[
            pl.BlockSpec((seq, MLA_HEAD_PAD), lambda b, h: (b, h)),
            pl.BlockSpec((seq, MLA_HEAD_PAD), lambda b, h: (b, h)),
            pl.BlockSpec((seq, MLA_V), lambda b, h: (b, h)),
        ],
        out_specs=pl.BlockSpec((seq, MLA_V), lambda b, h: (b, h)),
        out_shape=jax.ShapeDtypeStruct((t, heads * MLA_V), BF16),
        scratch_shapes=[pltpu.VMEM((seq, 2 * MLA_V), BF16)],
        compiler_params=_cparams(("parallel", "parallel"), 48),
        name="mla_attention",
    )(q, k, v)


def _mla_weights(w_in, w_uq, w_ukv, heads):
    q_lora = w_uq.shape[0]
    kv_lora = w_ukv.shape[0]
    wc = jnp.pad(w_in, [(0, 0), (0, LANES - MLA_ROPE)])
    uq = w_uq.reshape(q_lora, heads, MLA_DQK)
    wuq = jnp.pad(uq, [(0, 0), (0, 0), (0, MLA_HEAD_PAD - MLA_DQK)]).reshape(q_lora, heads * MLA_HEAD_PAD)
    ukv = w_ukv.reshape(kv_lora, heads, MLA_NOPE + MLA_V)
    wukv = jnp.concatenate([
        ukv[..., :MLA_NOPE].reshape(kv_lora, heads * MLA_NOPE),
        ukv[..., MLA_NOPE:].reshape(kv_lora, heads * MLA_V),
    ], axis=-1)
    return [w.astype(BF16) for w in (wc, wuq, wukv)]


def kernel(x, c, positions, norm_w, ada_w, ada_b, ret_w_in, ret_decay_logit_fwd,
           ret_decay_logit_bwd, ret_gn_w, ret_w_out, mla_w_in, mla_q_norm, mla_w_uq,
           mla_kv_norm, mla_w_ukv, mla_w_out, mlp_w1, mlp_w2):
    batch, seq, d = x.shape
    depth = norm_w.shape[0]
    assert depth == 2
    t = batch * seq
    ret_heads = ret_decay_logit_fwd.shape[1]
    mla_heads = mla_w_out.shape[1] // MLA_V

    mod = _ada_mod(c, ada_w, ada_b).reshape(depth, batch, 1, N_ADA * d)
    nw = norm_w.reshape(depth, 4, 1, d)
    inv_r, inv_m, sign_m = _rope_freqs()
    pos = positions.reshape(t, 1)
    x2 = x.reshape(t, d)

    proj = _ret_proj(x2, nw, mod, 0, ret_w_in[0].astype(BF16), pos, inv_r, seq, ret_heads)
    y = _retention(proj, ret_decay_logit_fwd[0], ret_decay_logit_bwd[0], ret_gn_w[0],
                   batch, seq, ret_heads)
    x2, h = _mixer_mlp(y, ret_w_out[0].astype(BF16), x2, mlp_w1[0].astype(BF16),
                       mlp_w2[0].astype(BF16), nw, mod, 0, seq, True)

    wc, wuq, wukv = _mla_weights(mla_w_in[0], mla_w_uq[0], mla_w_ukv[0], mla_heads)
    q, k, v = _mla_proj(h, wc, mla_q_norm[0], mla_kv_norm[0], wuq, wukv, pos, inv_m, sign_m,
                        mla_heads)
    o = _attention(q, k, v, batch, seq, mla_heads)
    (x2,) = _mixer_mlp(o, mla_w_out[0].astype(BF16), x2, mlp_w1[1].astype(BF16),
                       mlp_w2[1].astype(BF16), nw, mod, 1, seq, False)
    return x2.reshape(batch, seq, d)
```

```python
import functools

import numpy as np
import jax
import jax.numpy as jnp
from jax import lax
from jax.experimental import pallas as pl
from jax.experimental.pallas import tpu as pltpu

F32 = jnp.float32
BF16 = jnp.bfloat16

RMS_EPS = 1e-6
GN_EPS = 1e-5
ROPE_THETA = 10000.0
N_ADA = 6

RET_DK = 256
RET_DV = 512
RET_CHUNK = 256

MLA_NOPE = 128
MLA_ROPE = 64
MLA_V = 128
MLA_DQK = MLA_NOPE + MLA_ROPE
MLA_HEAD_PAD = 256

LANES = 128
MIB = 1024 * 1024

MLP_TM, MLP_SPLITS, MLP_TF, MLP_VMEM_MIB = 512, 2, 1024, 56
ATTN_HEADS_PER_STEP = 2


def _cparams(semantics, vmem_mib):
    return pltpu.CompilerParams(dimension_semantics=semantics,
                                vmem_limit_bytes=vmem_mib * MIB)


def _rms(t, w):
    return t * lax.rsqrt(jnp.mean(t * t, axis=-1, keepdims=True) + RMS_EPS) * w


def _dot(a, b):
    return jnp.dot(a, b, preferred_element_type=F32)


def _dot_nt(a, b):
    return lax.dot_general(a, b, (((1,), (1,)), ((), ())), preferred_element_type=F32)


def _dot_tn(a, b):
    return lax.dot_general(a, b, (((0,), (0,)), ((), ())), preferred_element_type=F32)


def _ada_kernel(c_ref, w_ref, b_ref, o_ref):
    c_act = jax.nn.silu(c_ref[...])
    o_ref[...] = _dot(c_act.astype(BF16), w_ref[...].astype(BF16)) + b_ref[...]


def _ada_mod(c, ada_w, ada_b):
    depth, d, n = ada_w.shape
    b = c.shape[0]
    tn = 1024
    return pl.pallas_call(
        _ada_kernel,
        grid=(depth, n // tn),
        in_specs=[
            pl.BlockSpec((b, d), lambda l, j: (0, 0)),
            pl.BlockSpec((None, d, tn), lambda l, j: (l, 0, j)),
            pl.BlockSpec((None, 1, tn), lambda l, j: (l, 0, j)),
        ],
        out_specs=pl.BlockSpec((None, b, tn), lambda l, j: (l, 0, j)),
        out_shape=jax.ShapeDtypeStruct((depth, b, n), F32),
        compiler_params=_cparams(("arbitrary", "arbitrary"), 32),
        name="ada_mod",
    )(c, ada_w, ada_b.reshape(depth, 1, n))


def _rope_freqs():
    half_m = MLA_ROPE // 2
    assert RET_DK // 2 == LANES
    inv_r = ROPE_THETA ** (-jnp.arange(0, RET_DK, 2, dtype=F32) / RET_DK)
    inv_m = ROPE_THETA ** (-jnp.arange(0, MLA_ROPE, 2, dtype=F32) / MLA_ROPE)
    inv_m = jnp.tile(inv_m, LANES // half_m)
    sign_m = jnp.concatenate([-jnp.ones((half_m,), F32), jnp.ones((half_m,), F32),
                              jnp.zeros((LANES - MLA_ROPE,), F32)])
    return inv_r.reshape(1, LANES), inv_m.reshape(1, LANES), sign_m.reshape(1, LANES)


def _ret_proj_kernel(x_ref, nw_ref, sc_ref, sh_ref, w_ref, pos_ref, inv_ref, o_ref,
                     *, hk, k_scale, splits, tn):
    rows_per = x_ref.shape[0] // splits
    half = RET_DK // 2
    for r in range(splits):
        rows = slice(r * rows_per, (r + 1) * rows_per)
        hn = _rms(x_ref[rows, :], nw_ref[...])
        h = (hn * (1.0 + sc_ref[...]) + sh_ref[...]).astype(BF16)
        ang = pos_ref[rows, :].astype(F32) * inv_ref[...]
        cos = jnp.cos(ang)
        sin = jnp.sin(ang)
        for j in range(w_ref.shape[1] // tn):
            p = _dot(h, w_ref[:, j * tn:(j + 1) * tn])
            if j * tn >= 2 * hk:
                o_ref[rows, j * tn:(j + 1) * tn] = p.astype(BF16)
                continue
            scale = k_scale if j * tn >= hk else 1.0
            for g in range(tn // RET_DK):
                lo = j * tn + g * RET_DK
                t1 = p[:, g * RET_DK:g * RET_DK + half] * scale
                t2 = p[:, g * RET_DK + half:(g + 1) * RET_DK] * scale
                o_ref[rows, lo:lo + half] = (t1 * cos - t2 * sin).astype(BF16)
                o_ref[rows, lo + half:lo + RET_DK] = (t1 * sin + t2 * cos).astype(BF16)


def _ret_proj(x2, nw, mod, layer, w_in, pos, inv_r, seq, heads):
    t, d = x2.shape
    n = w_in.shape[1]
    hk = heads * RET_DK
    tm, splits, tn = 512, 2, 1024
    assert hk % tn == 0
    per_b = seq // tm
    vec = lambda k: pl.BlockSpec((None, None, 1, d), lambda i: (layer, i // per_b, 0, k))
    return pl.pallas_call(
        functools.partial(_ret_proj_kernel, hk=hk, k_scale=RET_DK ** -0.5, splits=splits, tn=tn),
        grid=(t // tm,),
        in_specs=[
            pl.BlockSpec((tm, d), lambda i: (i, 0)),
            pl.BlockSpec((None, None, 1, d), lambda i: (layer, 0, 0, 0)),
            vec(1), vec(0),
            _resident((d, n)),
            pl.BlockSpec((tm, 1), lambda i: (i, 0)),
            _resident((1, LANES)),
        ],
        out_specs=pl.BlockSpec((tm, n), lambda i: (i, 0)),
        out_shape=jax.ShapeDtypeStruct((t, n), BF16),
        compiler_params=_cparams(("parallel",), 48),
        name="ret_proj",
    )(x2, nw, mod, mod, w_in, pos, inv_r)


def _retention_kernel(lf_ref, lb_ref, q_ref, k_ref, v_ref, g_ref, gn_ref, o_ref,
                      y_scr, sf_scr, sb_scr):
    c = RET_CHUNK
    n = q_ref.shape[0] // c
    lg_f = jax.nn.log_sigmoid(lf_ref[...].astype(F32))
    lg_b = jax.nn.log_sigmoid(lb_ref[...].astype(F32))
    row = lax.broadcasted_iota(jnp.int32, (c, c), 0).astype(F32)
    col = lax.broadcasted_iota(jnp.int32, (c, c), 1).astype(F32)
    pos = lax.broadcasted_iota(jnp.int32, (c, 1), 0).astype(F32)
    diff = row - col
    mask_f = diff >= 0
    mask_b = diff < 0
    decay = jnp.where(mask_f, jnp.exp(jnp.where(mask_f, diff * lg_f, 0.0)),
                      jnp.exp(jnp.where(mask_b, -diff * lg_b, 0.0)))
    cross_f = jnp.exp((pos + 1.0) * lg_f)
    cross_b = jnp.exp((c - pos) * lg_b)
    sdec_f = jnp.exp((c - 1.0 - pos) * lg_f)
    sdec_b = jnp.exp(pos * lg_b)
    cdec_f = jnp.exp(c * lg_f)
    cdec_b = jnp.exp(c * lg_b)
    gn_w = gn_ref[...]

    sf_scr[...] = jnp.zeros_like(sf_scr)
    sb_scr[...] = jnp.zeros_like(sb_scr)

    def rows(ci):
        return pl.ds(pl.multiple_of(ci * c, c), c)

    def step(ci, st_ref, cross, sdec, cdec, intra):
        r = rows(ci)
        qi, ki, vi = q_ref[r, :], k_ref[r, :], v_ref[r, :]
        st = st_ref[...]
        out = _dot(qi, st.astype(BF16)) * cross
        if intra:
            s = _dot_nt(qi, ki) * decay
            out = out + _dot(s.astype(BF16), vi)
        kd = (ki.astype(F32) * sdec).astype(BF16)
        st_ref[...] = st * cdec + _dot_tn(kd, vi)
        return out

    def finalize(ci, y):
        r = rows(ci)
        mu = jnp.mean(y, axis=-1, keepdims=True)
        yc = y - mu
        var = jnp.mean(yc * yc, axis=-1, keepdims=True)
        yn = yc * lax.rsqrt(var + GN_EPS) * gn_w
        o_ref[r, :] = (jax.nn.silu(g_ref[r, :].astype(F32)) * yn).astype(BF16)

    def first_half(i, carry):
        cb = n - 1 - i
        y_scr[rows(i), :] = step(i, sf_scr, cross_f, sdec_f, cdec_f, True)
        y_scr[rows(cb), :] = step(cb, sb_scr, cross_b, sdec_b, cdec_b, False)
        return carry

    def second_half(i, carry):
        cb = n - 1 - i
        finalize(i, y_scr[rows(i), :] + step(i, sf_scr, cross_f, sdec_f, cdec_f, True))
        finalize(cb, y_scr[rows(cb), :] + step(cb, sb_scr, cross_b, sdec_b, cdec_b, False))
        return carry

    lax.fori_loop(0, n // 2, first_half, 0, unroll=2)
    lax.fori_loop(n // 2, n, second_half, 0, unroll=2)


def _retention(proj, logit_f, logit_b, gn_w, batch, seq, heads):
    t = proj.shape[0]
    hk = heads * RET_DK
    hv = heads * RET_DV
    assert (seq // RET_CHUNK) % 4 == 0
    kb, vb, gb = hk // RET_DK, 2 * hk // RET_DV, (2 * hk + hv) // RET_DV
    scalar = pl.BlockSpec((None, 1, 1), lambda b, h: (h, 0, 0))
    return pl.pallas_call(
        _retention_kernel,
        grid=(batch, heads),
        in_specs=[
            scalar, scalar,
            pl.BlockSpec((seq, RET_DK), lambda b, h: (b, h)),
            pl.BlockSpec((seq, RET_DK), lambda b, h: (b, kb + h)),
            pl.BlockSpec((seq, RET_DV), lambda b, h: (b, vb + h)),
            pl.BlockSpec((seq, RET_DV), lambda b, h: (b, gb + h)),
            pl.BlockSpec((1, RET_DV), lambda b, h: (0, h)),
        ],
        out_specs=pl.BlockSpec((seq, RET_DV), lambda b, h: (b, h)),
        out_shape=jax.ShapeDtypeStruct((t, hv), BF16),
        scratch_shapes=[
            pltpu.VMEM((seq, RET_DV), F32),
            pltpu.VMEM((RET_DK, RET_DV), F32),
            pltpu.VMEM((RET_DK, RET_DV), F32),
        ],
        compiler_params=_cparams(("parallel", "parallel"), 48),
        name="retention",
    )(logit_f.reshape(heads, 1, 1), logit_b.reshape(heads, 1, 1), proj, proj, proj, proj,
      gn_w.reshape(1, hv))


def _mixer_mlp_kernel(*refs, emit_next, splits, tf):
    if emit_next:
        (a_ref, wo_ref, x_ref, ga_ref, nw1_ref, nw2_ref, scm_ref, shm_ref, w1_ref, w2_ref,
         gm_ref, nw3_ref, nwn_ref, scn_ref, shn_ref, xo_ref, hn_ref) = refs
    else:
        (a_ref, wo_ref, x_ref, ga_ref, nw1_ref, nw2_ref, scm_ref, shm_ref, w1_ref, w2_ref,
         gm_ref, nw3_ref, xo_ref) = refs
    rows_per = x_ref.shape[0] // splits
    dff = w1_ref.shape[1]
    for r in range(splits):
        rows = slice(r * rows_per, (r + 1) * rows_per)
        y = _dot(a_ref[rows, :], wo_ref[...])
        x1 = x_ref[rows, :] + ga_ref[...] * _rms(y, nw1_ref[...])
        h = (_rms(x1, nw2_ref[...]) * (1.0 + scm_ref[...]) + shm_ref[...]).astype(BF16)
        acc = None
        for j in range(dff // tf):
            cols = slice(j * tf, (j + 1) * tf)
            u = jnp.square(jnp.maximum(_dot(h, w1_ref[:, cols]), 0.0)).astype(BF16)
            part = _dot(u, w2_ref[cols, :])
            acc = part if acc is None else acc + part
        x2 = x1 + gm_ref[...] * _rms(acc, nw3_ref[...])
        xo_ref[rows, :] = x2
        if emit_next:
            hn = _rms(x2, nwn_ref[...]) * (1.0 + scn_ref[...]) + shn_ref[...]
            hn_ref[rows, :] = hn.astype(BF16)


def _resident(shape):
    return pl.BlockSpec(shape, lambda *_: (0,) * len(shape), pipeline_mode=pl.Buffered(1))


def _mixer_mlp(a, w_out, x2, w1, w2, nw, mod, layer, seq, emit_next):
    t, d = x2.shape
    kd = a.shape[1]
    dff = w1.shape[1]
    tm, splits, tf = MLP_TM, MLP_SPLITS, MLP_TF
    per_b = seq // tm
    vec = lambda l, k: pl.BlockSpec((None, None, 1, d), lambda i: (l, i // per_b, 0, k))
    nvec = lambda l, k: pl.BlockSpec((None, None, 1, d), lambda i: (l, k, 0, 0))
    row = pl.BlockSpec((tm, d), lambda i: (i, 0))
    in_specs = [
        pl.BlockSpec((tm, kd), lambda i: (i, 0)), _resident((kd, d)), row,
        vec(layer, 2), nvec(layer, 1), nvec(layer, 2), vec(layer, 4), vec(layer, 3),
        _resident((d, dff)), _resident((dff, d)), vec(layer, 5), nvec(layer, 3),
    ]
    args = [a, w_out, x2, mod, nw, nw, mod, mod, w1, w2, mod, nw]
    out_specs = [row]
    out_shape = [jax.ShapeDtypeStruct((t, d), F32)]
    if emit_next:
        in_specs += [nvec(layer + 1, 0), vec(layer + 1, 1), vec(layer + 1, 0)]
        args += [nw, mod, mod]
        out_specs.append(row)
        out_shape.append(jax.ShapeDtypeStruct((t, d), BF16))
    return pl.pallas_call(
        functools.partial(_mixer_mlp_kernel, emit_next=emit_next, splits=splits, tf=tf),
        grid=(t // tm,),
        in_specs=in_specs,
        out_specs=out_specs,
        out_shape=out_shape,
        compiler_params=_cparams(("parallel",), MLP_VMEM_MIB),
        name="mixer_mlp_next" if emit_next else "mixer_mlp_last",
    )(*args)


def _mla_proj_kernel(h_ref, wc_ref, qn_ref, kvn_ref, wuq_ref, wukv_ref, pos_ref, inv_ref,
                     sign_ref, q_ref, k_ref, v_ref, *, heads, splits):
    q_lora = qn_ref.shape[1]
    kv_lora = kvn_ref.shape[1]
    nope_w = heads * MLA_NOPE
    rows_per = h_ref.shape[0] // splits
    lane = lax.broadcasted_iota(jnp.int32, (1, LANES), 1)
    first_half = lane < MLA_ROPE // 2
    pair = rows_per // 2

    for r in range(splits):
        rows = slice(r * rows_per, (r + 1) * rows_per)
        pos_a = pos_ref[r * rows_per:r * rows_per + pair, :]
        pos_b = pos_ref[r * rows_per + pair:(r + 1) * rows_per, :]
        ang = jnp.where(lane < MLA_ROPE, pos_a, pos_b).astype(F32) * inv_ref[...]
        cos2 = jnp.cos(ang)
        sin2 = jnp.sin(ang)
        cos = jnp.concatenate([cos2, pltpu.roll(cos2, MLA_ROPE, 1)], axis=0)
        sin = jnp.concatenate([sin2, pltpu.roll(sin2, MLA_ROPE, 1)], axis=0) * sign_ref[...]

        def rotary(t, cos=cos, sin=sin):
            partner = jnp.where(first_half, pltpu.roll(t, LANES - MLA_ROPE // 2, 1),
                                pltpu.roll(t, MLA_ROPE // 2, 1))
            return t * cos + partner * sin

        c_all = _dot(h_ref[rows, :], wc_ref[...])
        cq = _rms(c_all[:, :q_lora], qn_ref[...]).astype(BF16)
        ckv = _rms(c_all[:, q_lora:q_lora + kv_lora], kvn_ref[...]).astype(BF16)
        kr = rotary(c_all[:, q_lora + kv_lora:]).astype(BF16)
        kva = _dot(ckv, wukv_ref[...])
        v_ref[rows, :] = kva[:, nope_w:].astype(BF16)
        for hd in range(heads):
            lo = hd * MLA_HEAD_PAD
            qh = _dot(cq, wuq_ref[:, lo:lo + MLA_HEAD_PAD])
            q_ref[rows, lo:lo + LANES] = qh[:, :LANES].astype(BF16)
            q_ref[rows, lo + LANES:lo + 2 * LANES] = rotary(qh[:, LANES:]).astype(BF16)
            k_ref[rows, lo:lo + LANES] = kva[:, hd * LANES:(hd + 1) * LANES].astype(BF16)
            k_ref[rows, lo + LANES:lo + 2 * LANES] = kr


def _mla_proj(h, wc, q_norm, kv_norm, wuq, wukv, pos, inv_m, sign_m, heads):
    t, d = h.shape
    tm, splits = 1024, 2
    qn = q_norm.reshape(1, -1) * float(MLA_DQK ** -0.5 * np.log2(np.e))
    kvn = kv_norm.reshape(1, -1)
    row = lambda w: pl.BlockSpec((tm, w), lambda i: (i, 0))
    return pl.pallas_call(
        functools.partial(_mla_proj_kernel, heads=heads, splits=splits),
        grid=(t // tm,),
        in_specs=[row(d), _resident(wc.shape), _resident(qn.shape), _resident(kvn.shape),
                  _resident(wuq.shape), _resident(wukv.shape), row(1), _resident((1, LANES)),
                  _resident((1, LANES))],
        out_specs=[row(heads * MLA_HEAD_PAD), row(heads * MLA_HEAD_PAD), row(heads * MLA_V)],
        out_shape=[jax.ShapeDtypeStruct((t, heads * MLA_HEAD_PAD), BF16),
                   jax.ShapeDtypeStruct((t, heads * MLA_HEAD_PAD), BF16),
                   jax.ShapeDtypeStruct((t, heads * MLA_V), BF16)],
        compiler_params=_cparams(("parallel",), 48),
        name="mla_proj",
    )(h, wc, qn, kvn, wuq, wukv, pos, inv_m, sign_m)


def _attn_kernel(q_ref, k_ref, v_ref, o_ref, v1_scr, *, sub, heads_per_step):
    seq = q_ref.shape[0]
    for hh in range(heads_per_step):
        qk = slice(hh * MLA_HEAD_PAD, (hh + 1) * MLA_HEAD_PAD)
        vo = slice(hh * MLA_V, (hh + 1) * MLA_V)
        v1 = slice(hh * 2 * MLA_V, (hh + 1) * 2 * MLA_V)
        v1_scr[:, hh * 2 * MLA_V:hh * 2 * MLA_V + MLA_V] = v_ref[:, vo]
        v1_scr[:, hh * 2 * MLA_V + MLA_V:(hh + 1) * 2 * MLA_V] = jnp.ones((seq, MLA_V), BF16)
        for r in range(seq // sub):
            rows = slice(r * sub, (r + 1) * sub)
            s = _dot_nt(q_ref[rows, qk], k_ref[:, qk])
            p = jnp.exp2(s - jnp.max(s, axis=-1, keepdims=True))
            ov = _dot(p.astype(BF16), v1_scr[:, v1])
            o_ref[rows, vo] = (ov[:, :MLA_V] / ov[:, MLA_V:MLA_V + 1]).astype(BF16)


def _attention(q, k, v, batch, seq, heads):
    t = q.shape[0]
    sub, hp = 256, ATTN_HEADS_PER_STEP
    return pl.pallas_call(
        functools.partial(_attn_kernel, sub=sub, heads_per_step=hp),
        grid=(batch, heads // hp),
        in_specs=[
            pl.BlockSpec((seq, hp * MLA_HEAD_PAD), lambda b, h: (b, h)),
            pl.BlockSpec((seq, hp * MLA_HEAD_PAD), lambda b, h: (b, h)),
            pl.BlockSpec((seq, hp * MLA_V), lambda b, h: (b, h)),
        ],
        out_specs=pl.BlockSpec((seq, hp * MLA_V), lambda b, h: (b, h)),
        out_shape=jax.ShapeDtypeStruct((t, heads * MLA_V), BF16),
        scratch_shapes=[pltpu.VMEM((seq, hp * 2 * MLA_V), BF16)],
        compiler_params=_cparams(("parallel", "parallel"), 48),
        name="mla_attention",
    )(q, k, v)


def _mla_weights(w_in, w_uq, w_ukv, heads):
    q_lora = w_uq.shape[0]
    kv_lora = w_ukv.shape[0]
    wc = jnp.pad(w_in, [(0, 0), (0, LANES - MLA_ROPE)])
    uq = w_uq.reshape(q_lora, heads, MLA_DQK)
    wuq = jnp.pad(uq, [(0, 0), (0, 0), (0, MLA_HEAD_PAD - MLA_DQK)]).reshape(q_lora, heads * MLA_HEAD_PAD)
    ukv = w_ukv.reshape(kv_lora, heads, MLA_NOPE + MLA_V)
    wukv = jnp.concatenate([
        ukv[..., :MLA_NOPE].reshape(kv_lora, heads * MLA_NOPE),
        ukv[..., MLA_NOPE:].reshape(kv_lora, heads * MLA_V),
    ], axis=-1)
    return [w.astype(BF16) for w in (wc, wuq, wukv)]


def kernel(x, c, positions, norm_w, ada_w, ada_b, ret_w_in, ret_decay_logit_fwd,
           ret_decay_logit_bwd, ret_gn_w, ret_w_out, mla_w_in, mla_q_norm, mla_w_uq,
           mla_kv_norm, mla_w_ukv, mla_w_out, mlp_w1, mlp_w2):
    batch, seq, d = x.shape
    depth = norm_w.shape[0]
    assert depth == 2
    t = batch * seq
    ret_heads = ret_decay_logit_fwd.shape[1]
    mla_heads = mla_w_out.shape[1] // MLA_V

    mod = _ada_mod(c, ada_w, ada_b).reshape(depth, batch, 1, N_ADA * d)
    nw = norm_w.reshape(depth, 4, 1, d)
    inv_r, inv_m, sign_m = _rope_freqs()
    pos = positions.reshape(t, 1)
    x2 = x.reshape(t, d)

    proj = _ret_proj(x2, nw, mod, 0, ret_w_in[0].astype(BF16), pos, inv_r, seq, ret_heads)
    y = _retention(proj, ret_decay_logit_fwd[0], ret_decay_logit_bwd[0], ret_gn_w[0],
                   batch, seq, ret_heads)
    x2, h = _mixer_mlp(y, ret_w_out[0].astype(BF16), x2, mlp_w1[0].astype(BF16),
                       mlp_w2[0].astype(BF16), nw, mod, 0, seq, True)

    wc, wuq, wukv = _mla_weights(mla_w_in[0], mla_w_uq[0], mla_w_ukv[0], mla_heads)
    q, k, v = _mla_proj(h, wc, mla_q_norm[0], mla_kv_norm[0], wuq, wukv, pos, inv_m, sign_m,
                        mla_heads)
    o = _attention(q, k, v, batch, seq, mla_heads)
    (x2,) = _mixer_mlp(o, mla_w_out[0].astype(BF16), x2, mlp_w1[1].astype(BF16),
                       mlp_w2[1].astype(BF16), nw, mod, 1, seq, False)
    return x2.reshape(batch, seq, d)
```

```python
import functools

import numpy as np
import jax
import jax.numpy as jnp
from jax import lax
from jax.experimental import pallas as pl
from jax.experimental.pallas import tpu as pltpu

F32 = jnp.float32
BF16 = jnp.bfloat16

RMS_EPS = 1e-6
GN_EPS = 1e-5
ROPE_THETA = 10000.0
N_ADA = 6

RET_DK = 256
RET_DV = 512
RET_CHUNK = 256

MLA_NOPE = 128
MLA_ROPE = 64
MLA_V = 128
MLA_DQK = MLA_NOPE + MLA_ROPE
MLA_HEAD_PAD = 256

LANES = 128
MIB = 1024 * 1024

MLP_TM, MLP_SPLITS, MLP_TF, MLP_VMEM_MIB = 512, 2, 1024, 56
ATTN_HEADS_PER_STEP = 2


def _cparams(semantics, vmem_mib):
    return pltpu.CompilerParams(dimension_semantics=semantics,
                                vmem_limit_bytes=vmem_mib * MIB)


def _rms(t, w):
    return t * lax.rsqrt(jnp.mean(t * t, axis=-1, keepdims=True) + RMS_EPS) * w


def _dot(a, b):
    return jnp.dot(a, b, preferred_element_type=F32)


def _dot_nt(a, b):
    return lax.dot_general(a, b, (((1,), (1,)), ((), ())), preferred_element_type=F32)


def _dot_tn(a, b):
    return lax.dot_general(a, b, (((0,), (0,)), ((), ())), preferred_element_type=F32)


def _ada_kernel(c_ref, w_ref, b_ref, o_ref):
    c_act = jax.nn.silu(c_ref[...])
    o_ref[...] = _dot(c_act.astype(BF16), w_ref[...].astype(BF16)) + b_ref[...]


def _ada_mod(c, ada_w, ada_b):
    depth, d, n = ada_w.shape
    b = c.shape[0]
    tn = 1024
    return pl.pallas_call(
        _ada_kernel,
        grid=(depth, n // tn),
        in_specs=[
            pl.BlockSpec((b, d), lambda l, j: (0, 0)),
            pl.BlockSpec((None, d, tn), lambda l, j: (l, 0, j)),
            pl.BlockSpec((None, 1, tn), lambda l, j: (l, 0, j)),
        ],
        out_specs=pl.BlockSpec((None, b, tn), lambda l, j: (l, 0, j)),
        out_shape=jax.ShapeDtypeStruct((depth, b, n), F32),
        compiler_params=_cparams(("arbitrary", "arbitrary"), 32),
        name="ada_mod",
    )(c, ada_w, ada_b.reshape(depth, 1, n))


def _rope_freqs():
    half_m = MLA_ROPE // 2
    assert RET_DK // 2 == LANES
    inv_r = ROPE_THETA ** (-jnp.arange(0, RET_DK, 2, dtype=F32) / RET_DK)
    inv_m = ROPE_THETA ** (-jnp.arange(0, MLA_ROPE, 2, dtype=F32) / MLA_ROPE)
    inv_m = jnp.tile(inv_m, LANES // half_m)
    sign_m = jnp.concatenate([-jnp.ones((half_m,), F32), jnp.ones((half_m,), F32),
                              jnp.zeros((LANES - MLA_ROPE,), F32)])
    return inv_r.reshape(1, LANES), inv_m.reshape(1, LANES), sign_m.reshape(1, LANES)


def _ret_proj_kernel(x_ref, nw_ref, sc_ref, sh_ref, w_ref, pos_ref, inv_ref, lf_ref, lb_ref,
                     qk_ref, vg_ref, *, heads, k_scale, splits, tn):
    rows_per = x_ref.shape[0] // splits
    hk, hv = heads * RET_DK, heads * RET_DV
    half = RET_DK // 2
    local = lax.broadcasted_iota(jnp.int32, (rows_per, 1), 0).astype(F32)
    sdec_f = jnp.exp((rows_per - 1.0 - local) * jax.nn.log_sigmoid(lf_ref[...]))
    sdec_b = jnp.exp(local * jax.nn.log_sigmoid(lb_ref[...]))

    def rows(r):
        return slice(r * rows_per, (r + 1) * rows_per)

    def prologue(r):
        hn = _rms(x_ref[rows(r), :], nw_ref[...])
        h = (hn * (1.0 + sc_ref[...]) + sh_ref[...]).astype(BF16)
        ang = pos_ref[rows(r), :].astype(F32) * inv_ref[...]
        return h, jnp.cos(ang), jnp.sin(ang)

    def emit(r, col, p, cos, sin):
        if col >= 2 * hk:
            is_g = col >= 2 * hk + hv
            for g in range(tn // RET_DV):
                hd = (col - 2 * hk - is_g * hv) // RET_DV + g
                blk = p[:, g * RET_DV:(g + 1) * RET_DV]
                vg_ref[is_g * heads + hd, rows(r), :] = (jax.nn.silu(blk) if is_g else blk).astype(BF16)
        else:
            is_k = col >= hk
            for g in range(tn // RET_DK):
                hd = (col - is_k * hk) // RET_DK + g
                t1 = p[:, g * RET_DK:g * RET_DK + half]
                t2 = p[:, g * RET_DK + half:(g + 1) * RET_DK]
                if is_k:
                    t1, t2 = t1 * k_scale, t2 * k_scale
                r1 = t1 * cos - t2 * sin
                r2 = t1 * sin + t2 * cos
                qk_ref[is_k * heads + hd, rows(r), :half] = r1.astype(BF16)
                qk_ref[is_k * heads + hd, rows(r), half:] = r2.astype(BF16)
                if is_k:
                    for slab, dec in ((2 * heads + hd, sdec_f), (3 * heads + hd, sdec_b)):
                        d = dec[:, hd:hd + 1]
                        qk_ref[slab, rows(r), :half] = (r1 * d).astype(BF16)
                        qk_ref[slab, rows(r), half:] = (r2 * d).astype(BF16)

    order = ([c for c in range(0, 2 * hk, tn)] + [c for c in range(2 * hk + hv, 2 * hk + 2 * hv, tn)]
             + [c for c in range(2 * hk, 2 * hk + hv, tn)])
    cur = prologue(0)
    for r in range(splits):
        h, cos, sin = cur
        for idx, col in enumerate(order):
            p = _dot(h, w_ref[:, col:col + tn])
            if idx == 0 and r + 1 < splits:
                cur = prologue(r + 1)
            emit(r, col, p, cos, sin)


def _ret_proj(x2, nw, mod, layer, w_in, pos, inv_r, logit_f, logit_b, seq, heads):
    t, d = x2.shape
    n = w_in.shape[1]
    hk, hv = heads * RET_DK, heads * RET_DV
    tm, splits, tn = 512, 2, 1024
    assert hk % tn == 0 and hv % tn == 0 and tm // splits == RET_CHUNK and seq % tm == 0
    per_b = seq // tm
    vec = lambda k: pl.BlockSpec((None, None, 1, d), lambda i: (layer, i // per_b, 0, k))
    return pl.pallas_call(
        functools.partial(_ret_proj_kernel, heads=heads, k_scale=RET_DK ** -0.5, splits=splits, tn=tn),
        grid=(t // tm,),
        in_specs=[
            pl.BlockSpec((tm, d), lambda i: (i, 0)),
            pl.BlockSpec((None, None, 1, d), lambda i: (layer, 0, 0, 0)),
            vec(1), vec(0),
            _resident((d, n)),
            pl.BlockSpec((tm, 1), lambda i: (i, 0)),
            _resident((1, LANES)),
            _resident((1, heads)), _resident((1, heads)),
        ],
        out_specs=[pl.BlockSpec((4 * heads, tm, RET_DK), lambda i: (0, i, 0)),
                   pl.BlockSpec((2 * heads, tm, RET_DV), lambda i: (0, i, 0))],
        out_shape=[jax.ShapeDtypeStruct((4 * heads, t, RET_DK), BF16),
                   jax.ShapeDtypeStruct((2 * heads, t, RET_DV), BF16)],
        compiler_params=_cparams(("parallel",), 56),
        name="ret_proj",
    )(x2, nw, mod, mod, w_in, pos, inv_r, logit_f.reshape(1, heads), logit_b.reshape(1, heads))


def _retention_kernel(lf_ref, lb_ref, q_ref, k_ref, kf_ref, kb_ref, v_ref, sg_ref, gn_ref, o_ref,
                      y_scr, sf_scr, sb_scr):
    c = RET_CHUNK
    n = q_ref.shape[0] // c
    lg_f = jax.nn.log_sigmoid(lf_ref[...].astype(F32))
    lg_b = jax.nn.log_sigmoid(lb_ref[...].astype(F32))
    row = lax.broadcasted_iota(jnp.int32, (c, c), 0).astype(F32)
    col = lax.broadcasted_iota(jnp.int32, (c, c), 1).astype(F32)
    pos = lax.broadcasted_iota(jnp.int32, (c, 1), 0).astype(F32)
    diff = row - col
    mask_f = diff >= 0
    mask_b = diff < 0
    decay = jnp.where(mask_f, jnp.exp(jnp.where(mask_f, diff * lg_f, 0.0)),
                      jnp.exp(jnp.where(mask_b, -diff * lg_b, 0.0)))
    cross_f = jnp.exp((pos + 1.0) * lg_f)
    cross_b = jnp.exp((c - pos) * lg_b)
    cdec_f = jnp.exp(c * lg_f)
    cdec_b = jnp.exp(c * lg_b)
    gn_w = gn_ref[...]

    sf_scr[...] = jnp.zeros_like(sf_scr)
    sb_scr[...] = jnp.zeros_like(sb_scr)

    def rows(ci):
        return pl.ds(pl.multiple_of(ci * c, c), c)

    def step(ci, st_ref, kd_ref, cross, cdec, intra):
        r = rows(ci)
        qi, vi = q_ref[r, :], v_ref[r, :]
        st = st_ref[...]
        out = _dot(qi, st.astype(BF16)) * cross
        if intra:
            s = _dot_nt(qi, k_ref[r, :]) * decay
            out = out + _dot(s.astype(BF16), vi)
        st_ref[...] = st * cdec + _dot_tn(kd_ref[r, :], vi)
        return out

    def finalize(ci, y):
        r = rows(ci)
        mu = jnp.mean(y, axis=-1, keepdims=True)
        yc = y - mu
        var = jnp.mean(yc * yc, axis=-1, keepdims=True)
        yn = yc * lax.rsqrt(var + GN_EPS) * gn_w
        o_ref[r, :] = (sg_ref[r, :].astype(F32) * yn).astype(BF16)

    def first_half(i, carry):
        cb = n - 1 - i
        y_scr[rows(i), :] = step(i, sf_scr, kf_ref, cross_f, cdec_f, True)
        y_scr[rows(cb), :] = step(cb, sb_scr, kb_ref, cross_b, cdec_b, False)
        return carry

    def second_half(i, carry):
        cb = n - 1 - i
        finalize(i, y_scr[rows(i), :] + step(i, sf_scr, kf_ref, cross_f, cdec_f, True))
        finalize(cb, y_scr[rows(cb), :] + step(cb, sb_scr, kb_ref, cross_b, cdec_b, False))
        return carry

    lax.fori_loop(0, n // 2, first_half, 0, unroll=2)
    lax.fori_loop(n // 2, n, second_half, 0, unroll=2)


def _retention(qk, vg, logit_f, logit_b, gn_w, batch, seq, heads):
    t = qk.shape[1]
    hv = heads * RET_DV
    assert (seq // RET_CHUNK) % 4 == 0
    scalar = pl.BlockSpec((None, 1, 1), lambda b, h: (h, 0, 0))
    slab = lambda m, w: pl.BlockSpec((None, seq, w), lambda b, h: (m * heads + h, b, 0))
    return pl.pallas_call(
        _retention_kernel,
        grid=(batch, heads),
        in_specs=[
            scalar, scalar,
            slab(0, RET_DK), slab(1, RET_DK), slab(2, RET_DK), slab(3, RET_DK),
            slab(0, RET_DV), slab(1, RET_DV),
            pl.BlockSpec((1, RET_DV), lambda b, h: (0, h)),
        ],
        out_specs=pl.BlockSpec((seq, RET_DV), lambda b, h: (b, h)),
        out_shape=jax.ShapeDtypeStruct((t, hv), BF16),
        scratch_shapes=[
            pltpu.VMEM((seq, RET_DV), F32),
            pltpu.VMEM((RET_DK, RET_DV), F32),
            pltpu.VMEM((RET_DK, RET_DV), F32),
        ],
        compiler_params=_cparams(("parallel", "parallel"), 48),
        name="retention",
    )(logit_f.reshape(heads, 1, 1), logit_b.reshape(heads, 1, 1), qk, qk, qk, qk, vg, vg,
      gn_w.reshape(1, hv))


def _mixer_mlp_kernel(*refs, emit_next, splits, tf):
    if emit_next:
        (a_ref, wo_ref, x_ref, ga_ref, nw1_ref, nw2_ref, scm_ref, shm_ref, w1_ref, w2_ref,
         gm_ref, nw3_ref, nwn_ref, scn_ref, shn_ref, xo_ref, hn_ref) = refs
    else:
        (a_ref, wo_ref, x_ref, ga_ref, nw1_ref, nw2_ref, scm_ref, shm_ref, w1_ref, w2_ref,
         gm_ref, nw3_ref, xo_ref) = refs
    rows_per = x_ref.shape[0] // splits
    n_chunks = w1_ref.shape[1] // tf

    def rows(r):
        return slice(r * rows_per, (r + 1) * rows_per)

    def prologue(r):
        y = _dot(a_ref[rows(r), :], wo_ref[...])
        x1 = x_ref[rows(r), :] + ga_ref[...] * _rms(y, nw1_ref[...])
        h = (_rms(x1, nw2_ref[...]) * (1.0 + scm_ref[...]) + shm_ref[...]).astype(BF16)
        return x1, h

    def epilogue(r, x1, acc):
        x2 = x1 + gm_ref[...] * _rms(acc, nw3_ref[...])
        xo_ref[rows(r), :] = x2
        if emit_next:
            hn = _rms(x2, nwn_ref[...]) * (1.0 + scn_ref[...]) + shn_ref[...]
            hn_ref[rows(r), :] = hn.astype(BF16)

    cur = prologue(0)
    done = None
    for r in range(splits):
        x1, h = cur
        acc = None
        for j in range(n_chunks):
            cols = slice(j * tf, (j + 1) * tf)
            u = jnp.square(jnp.maximum(_dot(h, w1_ref[:, cols]), 0.0)).astype(BF16)
            part = _dot(u, w2_ref[cols, :])
            acc = part if acc is None else acc + part
            if j == 0 and r + 1 < splits:
                cur = prologue(r + 1)
            if j == min(1, n_chunks - 1) and done is not None:
                epilogue(*done)
                done = None
        done = (r, x1, acc)
    epilogue(*done)


def _resident(shape):
    return pl.BlockSpec(shape, lambda *_: (0,) * len(shape), pipeline_mode=pl.Buffered(1))


def _mixer_mlp(a, w_out, x2, w1, w2, nw, mod, layer, seq, emit_next):
    t, d = x2.shape
    kd = a.shape[1]
    dff = w1.shape[1]
    tm, splits, tf = MLP_TM, MLP_SPLITS, MLP_TF
    per_b = seq // tm
    vec = lambda l, k: pl.BlockSpec((None, None, 1, d), lambda i: (l, i // per_b, 0, k))
    nvec = lambda l, k: pl.BlockSpec((None, None, 1, d), lambda i: (l, k, 0, 0))
    row = pl.BlockSpec((tm, d), lambda i: (i, 0))
    in_specs = [
        pl.BlockSpec((tm, kd), lambda i: (i, 0)), _resident((kd, d)), row,
        vec(layer, 2), nvec(layer, 1), nvec(layer, 2), vec(layer, 4), vec(layer, 3),
        _resident((d, dff)), _resident((dff, d)), vec(layer, 5), nvec(layer, 3),
    ]
    args = [a, w_out, x2, mod, nw, nw, mod, mod, w1, w2, mod, nw]
    out_specs = [row]
    out_shape = [jax.ShapeDtypeStruct((t, d), F32)]
    if emit_next:
        in_specs += [nvec(layer + 1, 0), vec(layer + 1, 1), vec(layer + 1, 0)]
        args += [nw, mod, mod]
        out_specs.append(row)
        out_shape.append(jax.ShapeDtypeStruct((t, d), BF16))
    return pl.pallas_call(
        functools.partial(_mixer_mlp_kernel, emit_next=emit_next, splits=splits, tf=tf),
        grid=(t // tm,),
        in_specs=in_specs,
        out_specs=out_specs,
        out_shape=out_shape,
        compiler_params=_cparams(("parallel",), MLP_VMEM_MIB),
        name="mixer_mlp_next" if emit_next else "mixer_mlp_last",
    )(*args)


def _mla_proj_kernel(h_ref, wc_ref, qn_ref, kvn_ref, wuq_ref, wukv_ref, pos_ref, inv_ref,
                     sign_ref, q_ref, k_ref, v_ref, *, heads, splits):
    q_lora = qn_ref.shape[1]
    kv_lora = kvn_ref.shape[1]
    nope_w = heads * MLA_NOPE
    rows_per = h_ref.shape[0] // splits
    lane = lax.broadcasted_iota(jnp.int32, (1, LANES), 1)
    first_half = lane < MLA_ROPE // 2
    pair = rows_per // 2

    def compress(r):
        rows = slice(r * rows_per, (r + 1) * rows_per)
        c_all = _dot(h_ref[rows, :], wc_ref[...])
        cq = _rms(c_all[:, :q_lora], qn_ref[...]).astype(BF16)
        ckv = _rms(c_all[:, q_lora:q_lora + kv_lora], kvn_ref[...]).astype(BF16)
        pos_a = pos_ref[r * rows_per:r * rows_per + pair, :]
        pos_b = pos_ref[r * rows_per + pair:(r + 1) * rows_per, :]
        ang = jnp.where(lane < MLA_ROPE, pos_a, pos_b).astype(F32) * inv_ref[...]
        cos2 = jnp.cos(ang)
        sin2 = jnp.sin(ang)
        cos = jnp.concatenate([cos2, pltpu.roll(cos2, MLA_ROPE, 1)], axis=0)
        sin = jnp.concatenate([sin2, pltpu.roll(sin2, MLA_ROPE, 1)], axis=0) * sign_ref[...]
        return cq, ckv, c_all[:, q_lora + kv_lora:], cos, sin

    cur = compress(0)
    for r in range(splits):
        rows = slice(r * rows_per, (r + 1) * rows_per)
        cq, ckv, kr_raw, cos, sin = cur

        def rotary(t, cos=cos, sin=sin):
            partner = jnp.where(first_half, pltpu.roll(t, LANES - MLA_ROPE // 2, 1),
                                pltpu.roll(t, MLA_ROPE // 2, 1))
            return t * cos + partner * sin

        kva = _dot(ckv, wukv_ref[...])
        if r + 1 < splits:
            cur = compress(r + 1)
        kr = rotary(kr_raw).astype(BF16)
        for hd in range(heads):
            lo = hd * MLA_HEAD_PAD
            qh = _dot(cq, wuq_ref[:, lo:lo + MLA_HEAD_PAD])
            q_ref[hd, rows, :LANES] = qh[:, :LANES].astype(BF16)
            q_ref[hd, rows, LANES:] = rotary(qh[:, LANES:]).astype(BF16)
            k_ref[hd, rows, :LANES] = kva[:, hd * LANES:(hd + 1) * LANES].astype(BF16)
            k_ref[hd, rows, LANES:] = kr
            v_ref[hd, rows, :] = kva[:, nope_w + hd * MLA_V:nope_w + (hd + 1) * MLA_V].astype(BF16)


def _mla_proj(h, wc, q_norm, kv_norm, wuq, wukv, pos, inv_m, sign_m, heads):
    t, d = h.shape
    tm, splits = 1024, 2
    qn = q_norm.reshape(1, -1) * float(MLA_DQK ** -0.5 * np.log2(np.e))
    kvn = kv_norm.reshape(1, -1)
    row = lambda w: pl.BlockSpec((tm, w), lambda i: (i, 0))
    return pl.pallas_call(
        functools.partial(_mla_proj_kernel, heads=heads, splits=splits),
        grid=(t // tm,),
        in_specs=[row(d), _resident(wc.shape), _resident(qn.shape), _resident(kvn.shape),
                  _resident(wuq.shape), _resident(wukv.shape), row(1), _resident((1, LANES)),
                  _resident((1, LANES))],
        out_specs=[pl.BlockSpec((heads, tm, w), lambda i: (0, i, 0))
                   for w in (MLA_HEAD_PAD, MLA_HEAD_PAD, MLA_V)],
        out_shape=[jax.ShapeDtypeStruct((heads, t, w), BF16)
                   for w in (MLA_HEAD_PAD, MLA_HEAD_PAD, MLA_V)],
        compiler_params=_cparams(("parallel",), 48),
        name="mla_proj",
    )(h, wc, qn, kvn, wuq, wukv, pos, inv_m, sign_m)


def _attn_kernel(q_ref, k_ref, v_ref, o_ref, v1_scr, *, sub, heads_per_step):
    seq = q_ref.shape[1]
    for hh in range(heads_per_step):
        vo = slice(hh * MLA_V, (hh + 1) * MLA_V)
        v1_scr[hh, :, :MLA_V] = v_ref[hh]
        v1_scr[hh, :, MLA_V:] = jnp.ones((seq, MLA_V), BF16)
        for r in range(seq // sub):
            rows = slice(r * sub, (r + 1) * sub)
            s = _dot_nt(q_ref[hh, rows, :], k_ref[hh])
            p = jnp.exp2(s - jnp.max(s, axis=-1, keepdims=True))
            ov = _dot(p.astype(BF16), v1_scr[hh])
            o_ref[rows, vo] = (ov[:, :MLA_V] / ov[:, MLA_V:MLA_V + 1]).astype(BF16)


def _attention(q, k, v, batch, seq, heads):
    t = q.shape[1]
    sub, hp = 256, ATTN_HEADS_PER_STEP
    slab = lambda w: pl.BlockSpec((hp, seq, w), lambda b, h: (h, b, 0))
    return pl.pallas_call(
        functools.partial(_attn_kernel, sub=sub, heads_per_step=hp),
        grid=(batch, heads // hp),
        in_specs=[slab(MLA_HEAD_PAD), slab(MLA_HEAD_PAD), slab(MLA_V)],
        out_specs=pl.BlockSpec((seq, hp * MLA_V), lambda b, h: (b, h)),
        out_shape=jax.ShapeDtypeStruct((t, heads * MLA_V), BF16),
        scratch_shapes=[pltpu.VMEM((hp, seq, 2 * MLA_V), BF16)],
        compiler_params=_cparams(("parallel", "parallel"), 48),
        name="mla_attention",
    )(q, k, v)


def _mla_weights(w_in, w_uq, w_ukv, heads):
    q_lora = w_uq.shape[0]
    kv_lora = w_ukv.shape[0]
    wc = jnp.pad(w_in, [(0, 0), (0, LANES - MLA_ROPE)])
    uq = w_uq.reshape(q_lora, heads, MLA_DQK)
    wuq = jnp.pad(uq, [(0, 0), (0, 0), (0, MLA_HEAD_PAD - MLA_DQK)]).reshape(q_lora, heads * MLA_HEAD_PAD)
    ukv = w_ukv.reshape(kv_lora, heads, MLA_NOPE + MLA_V)
    wukv = jnp.concatenate([
        ukv[..., :MLA_NOPE].reshape(kv_lora, heads * MLA_NOPE),
        ukv[..., MLA_NOPE:].reshape(kv_lora, heads * MLA_V),
    ], axis=-1)
    return [w.astype(BF16) for w in (wc, wuq, wukv)]


def kernel(x, c, positions, norm_w, ada_w, ada_b, ret_w_in, ret_decay_logit_fwd,
           ret_decay_logit_bwd, ret_gn_w, ret_w_out, mla_w_in, mla_q_norm, mla_w_uq,
           mla_kv_norm, mla_w_ukv, mla_w_out, mlp_w1, mlp_w2):
    batch, seq, d = x.shape
    depth = norm_w.shape[0]
    assert depth == 2
    t = batch * seq
    ret_heads = ret_decay_logit_fwd.shape[1]
    mla_heads = mla_w_out.shape[1] // MLA_V

    mod = _ada_mod(c, ada_w, ada_b).reshape(depth, batch, 1, N_ADA * d)
    nw = norm_w.reshape(depth, 4, 1, d)
    inv_r, inv_m, sign_m = _rope_freqs()
    pos = positions.reshape(t, 1)
    x2 = x.reshape(t, d)

    qk, vg = _ret_proj(x2, nw, mod, 0, ret_w_in[0].astype(BF16), pos, inv_r,
                       ret_decay_logit_fwd[0], ret_decay_logit_bwd[0], seq, ret_heads)
    y = _retention(qk, vg, ret_decay_logit_fwd[0], ret_decay_logit_bwd[0], ret_gn_w[0],
                   batch, seq, ret_heads)
    x2, h = _mixer_mlp(y, ret_w_out[0].astype(BF16), x2, mlp_w1[0].astype(BF16),
                       mlp_w2[0].astype(BF16), nw, mod, 0, seq, True)

    wc, wuq, wukv = _mla_weights(mla_w_in[0], mla_w_uq[0], mla_w_ukv[0], mla_heads)
    q, k, v = _mla_proj(h, wc, mla_q_norm[0], mla_kv_norm[0], wuq, wukv, pos, inv_m, sign_m,
                        mla_heads)
    o = _attention(q, k, v, batch, seq, mla_heads)
    (x2,) = _mixer_mlp(o, mla_w_out[0].astype(BF16), x2, mlp_w1[1].astype(BF16),
                       mlp_w2[1].astype(BF16), nw, mod, 1, seq, False)
    return x2.reshape(batch, seq, d)
```

```python
import functools

import numpy as np
import jax
import jax.numpy as jnp
from jax import lax
from jax.experimental import pallas as pl
from jax.experimental.pallas import tpu as pltpu

F32 = jnp.float32
BF16 = jnp.bfloat16

RMS_EPS = 1e-6
GN_EPS = 1e-5
ROPE_THETA = 10000.0
N_ADA = 6

RET_DK = 256
RET_DV = 512
RET_CHUNK = 256

MLA_NOPE = 128
MLA_ROPE = 64
MLA_V = 128
MLA_DQK = MLA_NOPE + MLA_ROPE
MLA_HEAD_PAD = 256

LANES = 128
MIB = 1024 * 1024

MLP_TM, MLP_SPLITS, MLP_TF, MLP_VMEM_MIB = 512, 2, 1024, 56
ATTN_HEADS_PER_STEP = 2


def _cparams(semantics, vmem_mib):
    return pltpu.CompilerParams(dimension_semantics=semantics,
                                vmem_limit_bytes=vmem_mib * MIB)


def _rms(t, w):
    return t * lax.rsqrt(jnp.mean(t * t, axis=-1, keepdims=True) + RMS_EPS) * w


def _dot(a, b):
    return jnp.dot(a, b, preferred_element_type=F32)


def _dot_nt(a, b):
    return lax.dot_general(a, b, (((1,), (1,)), ((), ())), preferred_element_type=F32)


def _dot_tn(a, b):
    return lax.dot_general(a, b, (((0,), (0,)), ((), ())), preferred_element_type=F32)


def _ada_kernel(c_ref, w_ref, b_ref, o_ref):
    c_act = jax.nn.silu(c_ref[...])
    o_ref[...] = _dot(c_act.astype(BF16), w_ref[...].astype(BF16)) + b_ref[...]


def _ada_mod(c, ada_w, ada_b):
    depth, d, n = ada_w.shape
    b = c.shape[0]
    tn = 1024
    return pl.pallas_call(
        _ada_kernel,
        grid=(depth, n // tn),
        in_specs=[
            pl.BlockSpec((b, d), lambda l, j: (0, 0)),
            pl.BlockSpec((None, d, tn), lambda l, j: (l, 0, j)),
            pl.BlockSpec((None, 1, tn), lambda l, j: (l, 0, j)),
        ],
        out_specs=pl.BlockSpec((None, b, tn), lambda l, j: (l, 0, j)),
        out_shape=jax.ShapeDtypeStruct((depth, b, n), F32),
        compiler_params=_cparams(("arbitrary", "arbitrary"), 32),
        name="ada_mod",
    )(c, ada_w, ada_b.reshape(depth, 1, n))


def _rope_freqs():
    half_m = MLA_ROPE // 2
    assert RET_DK // 2 == LANES
    inv_r = ROPE_THETA ** (-jnp.arange(0, RET_DK, 2, dtype=F32) / RET_DK)
    inv_m = ROPE_THETA ** (-jnp.arange(0, MLA_ROPE, 2, dtype=F32) / MLA_ROPE)
    inv_m = jnp.tile(inv_m, LANES // half_m)
    sign_m = jnp.concatenate([-jnp.ones((half_m,), F32), jnp.ones((half_m,), F32),
                              jnp.zeros((LANES - MLA_ROPE,), F32)])
    return inv_r.reshape(1, LANES), inv_m.reshape(1, LANES), sign_m.reshape(1, LANES)


def _ret_proj_kernel(x_ref, nw_ref, sc_ref, sh_ref, w_ref, pos_ref, inv_ref, lf_ref, lb_ref,
                     qk_ref, vg_ref, *, heads, k_scale, splits, tn):
    rows_per = x_ref.shape[0] // splits
    hk, hv = heads * RET_DK, heads * RET_DV
    half = RET_DK // 2
    local = lax.broadcasted_iota(jnp.int32, (rows_per, 1), 0).astype(F32)
    sdec_f = jnp.exp((rows_per - 1.0 - local) * jax.nn.log_sigmoid(lf_ref[...]))
    sdec_b = jnp.exp(local * jax.nn.log_sigmoid(lb_ref[...]))

    def rows(r):
        return slice(r * rows_per, (r + 1) * rows_per)

    def prologue(r):
        hn = _rms(x_ref[rows(r), :], nw_ref[...])
        h = (hn * (1.0 + sc_ref[...]) + sh_ref[...]).astype(BF16)
        ang = pos_ref[rows(r), :].astype(F32) * inv_ref[...]
        return h, jnp.cos(ang), jnp.sin(ang)

    def emit(r, col, p, cos, sin):
        if col >= 2 * hk:
            is_g = col >= 2 * hk + hv
            for g in range(tn // RET_DV):
                hd = (col - 2 * hk - is_g * hv) // RET_DV + g
                blk = p[:, g * RET_DV:(g + 1) * RET_DV]
                vg_ref[is_g * heads + hd, rows(r), :] = (jax.nn.silu(blk) if is_g else blk).astype(BF16)
        else:
            is_k = col >= hk
            for g in range(tn // RET_DK):
                hd = (col - is_k * hk) // RET_DK + g
                t1 = p[:, g * RET_DK:g * RET_DK + half]
                t2 = p[:, g * RET_DK + half:(g + 1) * RET_DK]
                if is_k:
                    t1, t2 = t1 * k_scale, t2 * k_scale
                r1 = t1 * cos - t2 * sin
                r2 = t1 * sin + t2 * cos
                qk_ref[is_k * heads + hd, rows(r), :half] = r1.astype(BF16)
                qk_ref[is_k * heads + hd, rows(r), half:] = r2.astype(BF16)
                if is_k:
                    for slab, dec in ((2 * heads + hd, sdec_f), (3 * heads + hd, sdec_b)):
                        d = dec[:, hd:hd + 1]
                        qk_ref[slab, rows(r), :half] = (r1 * d).astype(BF16)
                        qk_ref[slab, rows(r), half:] = (r2 * d).astype(BF16)

    order = ([c for c in range(0, 2 * hk, tn)] + [c for c in range(2 * hk + hv, 2 * hk + 2 * hv, tn)]
             + [c for c in range(2 * hk, 2 * hk + hv, tn)])
    cur = prologue(0)
    for r in range(splits):
        h, cos, sin = cur
        for idx, col in enumerate(order):
            p = _dot(h, w_ref[:, col:col + tn])
            if idx == 0 and r + 1 < splits:
                cur = prologue(r + 1)
            emit(r, col, p, cos, sin)


def _ret_proj(x2, nw, mod, layer, w_in, pos, inv_r, logit_f, logit_b, seq, heads):
    t, d = x2.shape
    n = w_in.shape[1]
    hk, hv = heads * RET_DK, heads * RET_DV
    tm, splits, tn = 512, 2, 1024
    assert hk % tn == 0 and hv % tn == 0 and tm // splits == RET_CHUNK and seq % tm == 0
    per_b = seq // tm
    vec = lambda k: pl.BlockSpec((None, None, 1, d), lambda i: (layer, i // per_b, 0, k))
    return pl.pallas_call(
        functools.partial(_ret_proj_kernel, heads=heads, k_scale=RET_DK ** -0.5, splits=splits, tn=tn),
        grid=(t // tm,),
        in_specs=[
            pl.BlockSpec((tm, d), lambda i: (i, 0)),
            pl.BlockSpec((None, None, 1, d), lambda i: (layer, 0, 0, 0)),
            vec(1), vec(0),
            _resident((d, n)),
            pl.BlockSpec((tm, 1), lambda i: (i, 0)),
            _resident((1, LANES)),
            _resident((1, heads)), _resident((1, heads)),
        ],
        out_specs=[pl.BlockSpec((4 * heads, tm, RET_DK), lambda i: (0, i, 0)),
                   pl.BlockSpec((2 * heads, tm, RET_DV), lambda i: (0, i, 0))],
        out_shape=[jax.ShapeDtypeStruct((4 * heads, t, RET_DK), BF16),
                   jax.ShapeDtypeStruct((2 * heads, t, RET_DV), BF16)],
        compiler_params=_cparams(("parallel",), 56),
        name="ret_proj",
    )(x2, nw, mod, mod, w_in, pos, inv_r, logit_f.reshape(1, heads), logit_b.reshape(1, heads))


def _retention_kernel(lf_ref, lb_ref, q_ref, k_ref, kf_ref, kb_ref, v_ref, sg_ref, gn_ref, o_ref,
                      sf_all, sb_all):
    c = RET_CHUNK
    n = q_ref.shape[0] // c
    lg_f = jax.nn.log_sigmoid(lf_ref[...].astype(F32))
    lg_b = jax.nn.log_sigmoid(lb_ref[...].astype(F32))
    row = lax.broadcasted_iota(jnp.int32, (c, c), 0).astype(F32)
    col = lax.broadcasted_iota(jnp.int32, (c, c), 1).astype(F32)
    pos = lax.broadcasted_iota(jnp.int32, (c, 1), 0).astype(F32)
    diff = row - col
    mask_f = diff >= 0
    mask_b = diff < 0
    decay = jnp.where(mask_f, jnp.exp(jnp.where(mask_f, diff * lg_f, 0.0)),
                      jnp.exp(jnp.where(mask_b, -diff * lg_b, 0.0)))
    cross_f = jnp.exp((pos + 1.0) * lg_f)
    cross_b = jnp.exp((c - pos) * lg_b)
    cdec_f = jnp.exp(c * lg_f)
    cdec_b = jnp.exp(c * lg_b)
    gn_w = gn_ref[...]

    def rows(ci):
        return slice(ci * c, (ci + 1) * c)

    sf = jnp.zeros(sf_all.shape[1:], F32)
    sb = jnp.zeros(sb_all.shape[1:], F32)
    for i in range(n):
        cb = n - 1 - i
        sf_all[i] = sf.astype(BF16)
        sb_all[cb] = sb.astype(BF16)
        if i + 1 < n:
            sf = sf * cdec_f + _dot_tn(kf_ref[rows(i), :], v_ref[rows(i), :])
            sb = sb * cdec_b + _dot_tn(kb_ref[rows(cb), :], v_ref[rows(cb), :])

    for ci in range(n):
        r = rows(ci)
        qi, vi = q_ref[r, :], v_ref[r, :]
        s = _dot_nt(qi, k_ref[r, :]) * decay
        y = (_dot(s.astype(BF16), vi) + _dot(qi, sf_all[ci]) * cross_f
             + _dot(qi, sb_all[ci]) * cross_b)
        mu = jnp.mean(y, axis=-1, keepdims=True)
        yc = y - mu
        var = jnp.mean(yc * yc, axis=-1, keepdims=True)
        yn = yc * lax.rsqrt(var + GN_EPS) * gn_w
        o_ref[r, :] = (sg_ref[r, :].astype(F32) * yn).astype(BF16)


def _retention(qk, vg, logit_f, logit_b, gn_w, batch, seq, heads):
    t = qk.shape[1]
    hv = heads * RET_DV
    assert (seq // RET_CHUNK) % 4 == 0
    scalar = pl.BlockSpec((None, 1, 1), lambda b, h: (h, 0, 0))
    slab = lambda m, w: pl.BlockSpec((None, seq, w), lambda b, h: (m * heads + h, b, 0))
    return pl.pallas_call(
        _retention_kernel,
        grid=(batch, heads),
        in_specs=[
            scalar, scalar,
            slab(0, RET_DK), slab(1, RET_DK), slab(2, RET_DK), slab(3, RET_DK),
            slab(0, RET_DV), slab(1, RET_DV),
            pl.BlockSpec((1, RET_DV), lambda b, h: (0, h)),
        ],
        out_specs=pl.BlockSpec((seq, RET_DV), lambda b, h: (b, h)),
        out_shape=jax.ShapeDtypeStruct((t, hv), BF16),
        scratch_shapes=[
            pltpu.VMEM((seq // RET_CHUNK, RET_DK, RET_DV), BF16),
            pltpu.VMEM((seq // RET_CHUNK, RET_DK, RET_DV), BF16),
        ],
        compiler_params=_cparams(("parallel", "parallel"), 48),
        name="retention",
    )(logit_f.reshape(heads, 1, 1), logit_b.reshape(heads, 1, 1), qk, qk, qk, qk, vg, vg,
      gn_w.reshape(1, hv))


def _mixer_mlp_kernel(*refs, emit_next, splits, tf):
    if emit_next:
        (a_ref, wo_ref, x_ref, ga_ref, nw1_ref, nw2_ref, scm_ref, shm_ref, w1_ref, w2_ref,
         gm_ref, nw3_ref, nwn_ref, scn_ref, shn_ref, xo_ref, hn_ref) = refs
    else:
        (a_ref, wo_ref, x_ref, ga_ref, nw1_ref, nw2_ref, scm_ref, shm_ref, w1_ref, w2_ref,
         gm_ref, nw3_ref, xo_ref) = refs
    rows_per = x_ref.shape[0] // splits
    n_chunks = w1_ref.shape[1] // tf

    def rows(r):
        return slice(r * rows_per, (r + 1) * rows_per)

    def prologue(r):
        y = _dot(a_ref[rows(r), :], wo_ref[...])
        x1 = x_ref[rows(r), :] + ga_ref[...] * _rms(y, nw1_ref[...])
        h = (_rms(x1, nw2_ref[...]) * (1.0 + scm_ref[...]) + shm_ref[...]).astype(BF16)
        return x1, h

    def epilogue(r, x1, acc):
        x2 = x1 + gm_ref[...] * _rms(acc, nw3_ref[...])
        xo_ref[rows(r), :] = x2
        if emit_next:
            hn = _rms(x2, nwn_ref[...]) * (1.0 + scn_ref[...]) + shn_ref[...]
            hn_ref[rows(r), :] = hn.astype(BF16)

    cur = prologue(0)
    done = None
    for r in range(splits):
        x1, h = cur
        acc = None
        for j in range(n_chunks):
            cols = slice(j * tf, (j + 1) * tf)
            u = jnp.square(jnp.maximum(_dot(h, w1_ref[:, cols]), 0.0)).astype(BF16)
            part = _dot(u, w2_ref[cols, :])
            acc = part if acc is None else acc + part
            if j == 0 and r + 1 < splits:
                cur = prologue(r + 1)
            if j == min(1, n_chunks - 1) and done is not None:
                epilogue(*done)
                done = None
        done = (r, x1, acc)
    epilogue(*done)


def _resident(shape):
    return pl.BlockSpec(shape, lambda *_: (0,) * len(shape), pipeline_mode=pl.Buffered(1))


def _mixer_mlp(a, w_out, x2, w1, w2, nw, mod, layer, seq, emit_next):
    t, d = x2.shape
    kd = a.shape[1]
    dff = w1.shape[1]
    tm, splits, tf = MLP_TM, MLP_SPLITS, MLP_TF
    per_b = seq // tm
    vec = lambda l, k: pl.BlockSpec((None, None, 1, d), lambda i: (l, i // per_b, 0, k))
    nvec = lambda l, k: pl.BlockSpec((None, None, 1, d), lambda i: (l, k, 0, 0))
    row = pl.BlockSpec((tm, d), lambda i: (i, 0))
    in_specs = [
        pl.BlockSpec((tm, kd), lambda i: (i, 0)), _resident((kd, d)), row,
        vec(layer, 2), nvec(layer, 1), nvec(layer, 2), vec(layer, 4), vec(layer, 3),
        _resident((d, dff)), _resident((dff, d)), vec(layer, 5), nvec(layer, 3),
    ]
    args = [a, w_out, x2, mod, nw, nw, mod, mod, w1, w2, mod, nw]
    out_specs = [row]
    out_shape = [jax.ShapeDtypeStruct((t, d), F32)]
    if emit_next:
        in_specs += [nvec(layer + 1, 0), vec(layer + 1, 1), vec(layer + 1, 0)]
        args += [nw, mod, mod]
        out_specs.append(row)
        out_shape.append(jax.ShapeDtypeStruct((t, d), BF16))
    return pl.pallas_call(
        functools.partial(_mixer_mlp_kernel, emit_next=emit_next, splits=splits, tf=tf),
        grid=(t // tm,),
        in_specs=in_specs,
        out_specs=out_specs,
        out_shape=out_shape,
        compiler_params=_cparams(("parallel",), MLP_VMEM_MIB),
        name="mixer_mlp_next" if emit_next else "mixer_mlp_last",
    )(*args)


def _mla_proj_kernel(h_ref, wc_ref, qn_ref, kvn_ref, wuq_ref, wukv_ref, pos_ref, inv_ref,
                     sign_ref, q_ref, k_ref, v_ref, *, heads, splits):
    q_lora = qn_ref.shape[1]
    kv_lora = kvn_ref.shape[1]
    nope_w = heads * MLA_NOPE
    rows_per = h_ref.shape[0] // splits
    lane = lax.broadcasted_iota(jnp.int32, (1, LANES), 1)
    first_half = lane < MLA_ROPE // 2
    pair = rows_per // 2

    def compress(r):
        rows = slice(r * rows_per, (r + 1) * rows_per)
        c_all = _dot(h_ref[rows, :], wc_ref[...])
        cq = _rms(c_all[:, :q_lora], qn_ref[...]).astype(BF16)
        ckv = _rms(c_all[:, q_lora:q_lora + kv_lora], kvn_ref[...]).astype(BF16)
        pos_a = pos_ref[r * rows_per:r * rows_per + pair, :]
        pos_b = pos_ref[r * rows_per + pair:(r + 1) * rows_per, :]
        ang = jnp.where(lane < MLA_ROPE, pos_a, pos_b).astype(F32) * inv_ref[...]
        cos2 = jnp.cos(ang)
        sin2 = jnp.sin(ang)
        cos = jnp.concatenate([cos2, pltpu.roll(cos2, MLA_ROPE, 1)], axis=0)
        sin = jnp.concatenate([sin2, pltpu.roll(sin2, MLA_ROPE, 1)], axis=0) * sign_ref[...]
        return cq, ckv, c_all[:, q_lora + kv_lora:], cos, sin

    cur = compress(0)
    for r in range(splits):
        rows = slice(r * rows_per, (r + 1) * rows_per)
        cq, ckv, kr_raw, cos, sin = cur

        def rotary(t, cos=cos, sin=sin):
            partner = jnp.where(first_half, pltpu.roll(t, LANES - MLA_ROPE // 2, 1),
                                pltpu.roll(t, MLA_ROPE // 2, 1))
            return t * cos + partner * sin

        kva = _dot(ckv, wukv_ref[...])
        if r + 1 < splits:
            cur = compress(r + 1)
        kr = rotary(kr_raw).astype(BF16)
        for hd in range(heads):
            lo = hd * MLA_HEAD_PAD
            qh = _dot(cq, wuq_ref[:, lo:lo + MLA_HEAD_PAD])
            q_ref[hd, rows, :LANES] = qh[:, :LANES].astype(BF16)
            q_ref[hd, rows, LANES:] = rotary(qh[:, LANES:]).astype(BF16)
            k_ref[hd, rows, :LANES] = kva[:, hd * LANES:(hd + 1) * LANES].astype(BF16)
            k_ref[hd, rows, LANES:] = kr
            v_ref[hd, rows, :] = kva[:, nope_w + hd * MLA_V:nope_w + (hd + 1) * MLA_V].astype(BF16)


def _mla_proj(h, wc, q_norm, kv_norm, wuq, wukv, pos, inv_m, sign_m, heads):
    t, d = h.shape
    tm, splits = 1024, 2
    qn = q_norm.reshape(1, -1) * float(MLA_DQK ** -0.5 * np.log2(np.e))
    kvn = kv_norm.reshape(1, -1)
    row = lambda w: pl.BlockSpec((tm, w), lambda i: (i, 0))
    return pl.pallas_call(
        functools.partial(_mla_proj_kernel, heads=heads, splits=splits),
        grid=(t // tm,),
        in_specs=[row(d), _resident(wc.shape), _resident(qn.shape), _resident(kvn.shape),
                  _resident(wuq.shape), _resident(wukv.shape), row(1), _resident((1, LANES)),
                  _resident((1, LANES))],
        out_specs=[pl.BlockSpec((heads, tm, w), lambda i: (0, i, 0))
                   for w in (MLA_HEAD_PAD, MLA_HEAD_PAD, MLA_V)],
        out_shape=[jax.ShapeDtypeStruct((heads, t, w), BF16)
                   for w in (MLA_HEAD_PAD, MLA_HEAD_PAD, MLA_V)],
        compiler_params=_cparams(("parallel",), 48),
        name="mla_proj",
    )(h, wc, qn, kvn, wuq, wukv, pos, inv_m, sign_m)


def _attn_kernel(q_ref, k_ref, v_ref, o_ref, v1_scr, *, sub, heads_per_step):
    seq = q_ref.shape[1]
    n_sub = seq // sub
    for hh in range(heads_per_step):
        v1_scr[hh, :, :MLA_V] = v_ref[hh]
        v1_scr[hh, :, MLA_V:] = jnp.ones((seq, MLA_V), BF16)

    def scores(i):
        hh, r = divmod(i, n_sub)
        return _dot_nt(q_ref[hh, r * sub:(r + 1) * sub, :], k_ref[hh])

    s_next = scores(0)
    for i in range(heads_per_step * n_sub):
        hh, r = divmod(i, n_sub)
        s = s_next
        if i + 1 < heads_per_step * n_sub:
            s_next = scores(i + 1)
        p = jnp.exp2(s - jnp.max(s, axis=-1, keepdims=True))
        ov = _dot(p.astype(BF16), v1_scr[hh])
        o_ref[r * sub:(r + 1) * sub, hh * MLA_V:(hh + 1) * MLA_V] = (
            ov[:, :MLA_V] / ov[:, MLA_V:MLA_V + 1]).astype(BF16)


def _attention(q, k, v, batch, seq, heads):
    t = q.shape[1]
    sub, hp = 256, ATTN_HEADS_PER_STEP
    slab = lambda w: pl.BlockSpec((hp, seq, w), lambda b, h: (h, b, 0))
    return pl.pallas_call(
        functools.partial(_attn_kernel, sub=sub, heads_per_step=hp),
        grid=(batch, heads // hp),
        in_specs=[slab(MLA_HEAD_PAD), slab(MLA_HEAD_PAD), slab(MLA_V)],
        out_specs=pl.BlockSpec((seq, hp * MLA_V), lambda b, h: (b, h)),
        out_shape=jax.ShapeDtypeStruct((t, heads * MLA_V), BF16),
        scratch_shapes=[pltpu.VMEM((hp, seq, 2 * MLA_V), BF16)],
        compiler_params=_cparams(("parallel", "parallel"), 48),
        name="mla_attention",
    )(q, k, v)


def _mla_weights(w_in, w_uq, w_ukv, heads):
    q_lora = w_uq.shape[0]
    kv_lora = w_ukv.shape[0]
    wc = jnp.pad(w_in, [(0, 0), (0, LANES - MLA_ROPE)])
    uq = w_uq.reshape(q_lora, heads, MLA_DQK)
    wuq = jnp.pad(uq, [(0, 0), (0, 0), (0, MLA_HEAD_PAD - MLA_DQK)]).reshape(q_lora, heads * MLA_HEAD_PAD)
    ukv = w_ukv.reshape(kv_lora, heads, MLA_NOPE + MLA_V)
    wukv = jnp.concatenate([
        ukv[..., :MLA_NOPE].reshape(kv_lora, heads * MLA_NOPE),
        ukv[..., MLA_NOPE:].reshape(kv_lora, heads * MLA_V),
    ], axis=-1)
    return [w.astype(BF16) for w in (wc, wuq, wukv)]


def kernel(x, c, positions, norm_w, ada_w, ada_b, ret_w_in, ret_decay_logit_fwd,
           ret_decay_logit_bwd, ret_gn_w, ret_w_out, mla_w_in, mla_q_norm, mla_w_uq,
           mla_kv_norm, mla_w_ukv, mla_w_out, mlp_w1, mlp_w2):
    batch, seq, d = x.shape
    depth = norm_w.shape[0]
    assert depth == 2
    t = batch * seq
    ret_heads = ret_decay_logit_fwd.shape[1]
    mla_heads = mla_w_out.shape[1] // MLA_V

    mod = _ada_mod(c, ada_w, ada_b).reshape(depth, batch, 1, N_ADA * d)
    nw = norm_w.reshape(depth, 4, 1, d)
    inv_r, inv_m, sign_m = _rope_freqs()
    pos = positions.reshape(t, 1)
    x2 = x.reshape(t, d)

    qk, vg = _ret_proj(x2, nw, mod, 0, ret_w_in[0].astype(BF16), pos, inv_r,
                       ret_decay_logit_fwd[0], ret_decay_logit_bwd[0], seq, ret_heads)
    y = _retention(qk, vg, ret_decay_logit_fwd[0], ret_decay_logit_bwd[0], ret_gn_w[0],
                   batch, seq, ret_heads)
    x2, h = _mixer_mlp(y, ret_w_out[0].astype(BF16), x2, mlp_w1[0].astype(BF16),
                       mlp_w2[0].astype(BF16), nw, mod, 0, seq, True)

    wc, wuq, wukv = _mla_weights(mla_w_in[0], mla_w_uq[0], mla_w_ukv[0], mla_heads)
    q, k, v = _mla_proj(h, wc, mla_q_norm[0], mla_kv_norm[0], wuq, wukv, pos, inv_m, sign_m,
                        mla_heads)
    o = _attention(q, k, v, batch, seq, mla_heads)
    (x2,) = _mixer_mlp(o, mla_w_out[0].astype(BF16), x2, mlp_w1[1].astype(BF16),
                       mlp_w2[1].astype(BF16), nw, mod, 1, seq, False)
    return x2.reshape(batch, seq, d)
```

```python
import functools

import numpy as np
import jax
import jax.numpy as jnp
from jax import lax
from jax.experimental import pallas as pl
from jax.experimental.pallas import tpu as pltpu

F32 = jnp.float32
BF16 = jnp.bfloat16

RMS_EPS = 1e-6
GN_EPS = 1e-5
ROPE_THETA = 10000.0
N_ADA = 6

RET_DK = 256
RET_DV = 512
RET_CHUNK = 256

MLA_NOPE = 128
MLA_ROPE = 64
MLA_V = 128
MLA_DQK = MLA_NOPE + MLA_ROPE
MLA_HEAD_PAD = 256

LANES = 128
MIB = 1024 * 1024

MLP_TM, MLP_SPLITS, MLP_TF, MLP_VMEM_MIB = 512, 2, 1024, 56
ATTN_HEADS_PER_STEP = 2


def _cparams(semantics, vmem_mib):
    return pltpu.CompilerParams(dimension_semantics=semantics,
                                vmem_limit_bytes=vmem_mib * MIB)


def _rms(t, w):
    return t * lax.rsqrt(jnp.mean(t * t, axis=-1, keepdims=True) + RMS_EPS) * w


def _dot(a, b):
    return jnp.dot(a, b, preferred_element_type=F32)


def _dot_nt(a, b):
    return lax.dot_general(a, b, (((1,), (1,)), ((), ())), preferred_element_type=F32)


def _dot_tn(a, b):
    return lax.dot_general(a, b, (((0,), (0,)), ((), ())), preferred_element_type=F32)


def _ada_kernel(c_ref, w_ref, b_ref, o_ref):
    c_act = jax.nn.silu(c_ref[...])
    o_ref[...] = _dot(c_act.astype(BF16), w_ref[...].astype(BF16)) + b_ref[...]


def _ada_mod(c, ada_w, ada_b):
    depth, d, n = ada_w.shape
    b = c.shape[0]
    tn = 1024
    return pl.pallas_call(
        _ada_kernel,
        grid=(depth, n // tn),
        in_specs=[
            pl.BlockSpec((b, d), lambda l, j: (0, 0)),
            pl.BlockSpec((None, d, tn), lambda l, j: (l, 0, j)),
            pl.BlockSpec((None, 1, tn), lambda l, j: (l, 0, j)),
        ],
        out_specs=pl.BlockSpec((None, b, tn), lambda l, j: (l, 0, j)),
        out_shape=jax.ShapeDtypeStruct((depth, b, n), F32),
        compiler_params=_cparams(("arbitrary", "arbitrary"), 32),
        name="ada_mod",
    )(c, ada_w, ada_b.reshape(depth, 1, n))


def _rope_freqs():
    half_m = MLA_ROPE // 2
    assert RET_DK // 2 == LANES
    inv_r = ROPE_THETA ** (-jnp.arange(0, RET_DK, 2, dtype=F32) / RET_DK)
    inv_m = ROPE_THETA ** (-jnp.arange(0, MLA_ROPE, 2, dtype=F32) / MLA_ROPE)
    inv_m = jnp.tile(inv_m, LANES // half_m)
    sign_m = jnp.concatenate([-jnp.ones((half_m,), F32), jnp.ones((half_m,), F32),
                              jnp.zeros((LANES - MLA_ROPE,), F32)])
    return inv_r.reshape(1, LANES), inv_m.reshape(1, LANES), sign_m.reshape(1, LANES)


def _ret_proj_kernel(x_ref, nw_ref, sc_ref, sh_ref, w_ref, pos_ref, inv_ref, lf_ref, lb_ref,
                     qk_ref, vg_ref, *, heads, k_scale, splits, tn):
    rows_per = x_ref.shape[0] // splits
    hk, hv = heads * RET_DK, heads * RET_DV
    half = RET_DK // 2
    local = lax.broadcasted_iota(jnp.int32, (rows_per, 1), 0).astype(F32)
    sdec_f = jnp.exp((rows_per - 1.0 - local) * jax.nn.log_sigmoid(lf_ref[...]))
    sdec_b = jnp.exp(local * jax.nn.log_sigmoid(lb_ref[...]))

    def rows(r):
        return slice(r * rows_per, (r + 1) * rows_per)

    pre_gain = nw_ref[...] * (1.0 + sc_ref[...])

    def prologue(r):
        h = (_rms(x_ref[rows(r), :], pre_gain) + sh_ref[...]).astype(BF16)
        ang = pos_ref[rows(r), :].astype(F32) * inv_ref[...]
        return h, jnp.cos(ang), jnp.sin(ang)

    def emit(r, col, p, cos, sin):
        if col >= 2 * hk:
            is_g = col >= 2 * hk + hv
            for g in range(tn // RET_DV):
                hd = (col - 2 * hk - is_g * hv) // RET_DV + g
                blk = p[:, g * RET_DV:(g + 1) * RET_DV]
                vg_ref[is_g * heads + hd, rows(r), :] = (jax.nn.silu(blk) if is_g else blk).astype(BF16)
        else:
            is_k = col >= hk
            if is_k:
                cos, sin = cos * k_scale, sin * k_scale
            for g in range(tn // RET_DK):
                hd = (col - is_k * hk) // RET_DK + g
                t1 = p[:, g * RET_DK:g * RET_DK + half]
                t2 = p[:, g * RET_DK + half:(g + 1) * RET_DK]
                r1 = t1 * cos - t2 * sin
                r2 = t1 * sin + t2 * cos
                qk_ref[is_k * heads + hd, rows(r), :half] = r1.astype(BF16)
                qk_ref[is_k * heads + hd, rows(r), half:] = r2.astype(BF16)
                if is_k:
                    for slab, dec in ((2 * heads + hd, sdec_f), (3 * heads + hd, sdec_b)):
                        d = dec[:, hd:hd + 1]
                        qk_ref[slab, rows(r), :half] = (r1 * d).astype(BF16)
                        qk_ref[slab, rows(r), half:] = (r2 * d).astype(BF16)

    order = ([c for c in range(0, 2 * hk, tn)] + [c for c in range(2 * hk + hv, 2 * hk + 2 * hv, tn)]
             + [c for c in range(2 * hk, 2 * hk + hv, tn)])
    cur = prologue(0)
    for r in range(splits):
        h, cos, sin = cur
        for idx, col in enumerate(order):
            p = _dot(h, w_ref[:, col:col + tn])
            if idx == 0 and r + 1 < splits:
                cur = prologue(r + 1)
            emit(r, col, p, cos, sin)


def _ret_proj(x2, nw, mod, layer, w_in, pos, inv_r, logit_f, logit_b, seq, heads):
    t, d = x2.shape
    n = w_in.shape[1]
    hk, hv = heads * RET_DK, heads * RET_DV
    tm, splits, tn = 512, 2, 1024
    assert hk % tn == 0 and hv % tn == 0 and tm // splits == RET_CHUNK and seq % tm == 0
    per_b = seq // tm
    vec = lambda k: pl.BlockSpec((None, None, 1, d), lambda i: (layer, i // per_b, 0, k))
    return pl.pallas_call(
        functools.partial(_ret_proj_kernel, heads=heads, k_scale=RET_DK ** -0.5, splits=splits, tn=tn),
        grid=(t // tm,),
        in_specs=[
            pl.BlockSpec((tm, d), lambda i: (i, 0)),
            pl.BlockSpec((None, None, 1, d), lambda i: (layer, 0, 0, 0)),
            vec(1), vec(0),
            _resident((d, n)),
            pl.BlockSpec((tm, 1), lambda i: (i, 0)),
            _resident((1, LANES)),
            _resident((1, heads)), _resident((1, heads)),
        ],
        out_specs=[pl.BlockSpec((4 * heads, tm, RET_DK), lambda i: (0, i, 0)),
                   pl.BlockSpec((2 * heads, tm, RET_DV), lambda i: (0, i, 0))],
        out_shape=[jax.ShapeDtypeStruct((4 * heads, t, RET_DK), BF16),
                   jax.ShapeDtypeStruct((2 * heads, t, RET_DV), BF16)],
        compiler_params=_cparams(("parallel",), 56),
        name="ret_proj",
    )(x2, nw, mod, mod, w_in, pos, inv_r, logit_f.reshape(1, heads), logit_b.reshape(1, heads))


def _retention_kernel(lf_ref, lb_ref, q_ref, k_ref, kf_ref, kb_ref, v_ref, sg_ref, gn_ref, o_ref,
                      sf_all, sb_all):
    c = RET_CHUNK
    n = q_ref.shape[0] // c
    lg_f = jax.nn.log_sigmoid(lf_ref[...].astype(F32))
    lg_b = jax.nn.log_sigmoid(lb_ref[...].astype(F32))
    row = lax.broadcasted_iota(jnp.int32, (c, c), 0).astype(F32)
    col = lax.broadcasted_iota(jnp.int32, (c, c), 1).astype(F32)
    pos = lax.broadcasted_iota(jnp.int32, (c, 1), 0).astype(F32)
    diff = row - col
    mask_f = diff >= 0
    mask_b = diff < 0
    decay = jnp.where(mask_f, jnp.exp(jnp.where(mask_f, diff * lg_f, 0.0)),
                      jnp.exp(jnp.where(mask_b, -diff * lg_b, 0.0)))
    cross_f = jnp.exp((pos + 1.0) * lg_f)
    cross_b = jnp.exp((c - pos) * lg_b)
    cdec_f = jnp.exp(c * lg_f)
    cdec_b = jnp.exp(c * lg_b)
    gn_w = gn_ref[...]

    def rows(ci):
        return slice(ci * c, (ci + 1) * c)

    sf = jnp.zeros(sf_all.shape[1:], F32)
    sb = jnp.zeros(sb_all.shape[1:], F32)
    for i in range(n):
        cb = n - 1 - i
        sf_all[i] = sf.astype(BF16)
        sb_all[cb] = sb.astype(BF16)
        if i + 1 < n:
            sf = sf * cdec_f + _dot_tn(kf_ref[rows(i), :], v_ref[rows(i), :])
            sb = sb * cdec_b + _dot_tn(kb_ref[rows(cb), :], v_ref[rows(cb), :])

    for ci in range(n):
        r = rows(ci)
        qi, vi = q_ref[r, :], v_ref[r, :]
        s = _dot_nt(qi, k_ref[r, :]) * decay
        y = (_dot(s.astype(BF16), vi) + _dot(qi, sf_all[ci]) * cross_f
             + _dot(qi, sb_all[ci]) * cross_b)
        mu = jnp.mean(y, axis=-1, keepdims=True)
        yc = y - mu
        var = jnp.mean(yc * yc, axis=-1, keepdims=True)
        yn = yc * lax.rsqrt(var + GN_EPS) * gn_w
        o_ref[r, :] = (sg_ref[r, :].astype(F32) * yn).astype(BF16)


def _retention(qk, vg, logit_f, logit_b, gn_w, batch, seq, heads):
    t = qk.shape[1]
    hv = heads * RET_DV
    scalar = pl.BlockSpec((None, 1, 1), lambda b, h: (h, 0, 0))
    slab = lambda m, w: pl.BlockSpec((None, seq, w), lambda b, h: (m * heads + h, b, 0))
    return pl.pallas_call(
        _retention_kernel,
        grid=(batch, heads),
        in_specs=[
            scalar, scalar,
            *[slab(m, RET_DK) for m in range(4)],
            slab(0, RET_DV), slab(1, RET_DV),
            pl.BlockSpec((1, RET_DV), lambda b, h: (0, h)),
        ],
        out_specs=pl.BlockSpec((seq, RET_DV), lambda b, h: (b, h)),
        out_shape=jax.ShapeDtypeStruct((t, hv), BF16),
        scratch_shapes=[
            pltpu.VMEM((seq // RET_CHUNK, RET_DK, RET_DV), BF16),
            pltpu.VMEM((seq // RET_CHUNK, RET_DK, RET_DV), BF16),
        ],
        compiler_params=_cparams(("parallel", "parallel"), 48),
        name="retention",
    )(logit_f.reshape(heads, 1, 1), logit_b.reshape(heads, 1, 1), *([qk] * 4), vg, vg,
      gn_w.reshape(1, hv))


def _mixer_mlp_kernel(*refs, emit_next, splits, tf):
    if emit_next:
        (a_ref, wo_ref, x_ref, ga_ref, nw1_ref, nw2_ref, scm_ref, shm_ref, w1_ref, w2_ref,
         gm_ref, nw3_ref, nwn_ref, scn_ref, shn_ref, xo_ref, hn_ref) = refs
    else:
        (a_ref, wo_ref, x_ref, ga_ref, nw1_ref, nw2_ref, scm_ref, shm_ref, w1_ref, w2_ref,
         gm_ref, nw3_ref, xo_ref) = refs
    rows_per = x_ref.shape[0] // splits
    n_chunks = w1_ref.shape[1] // tf

    def rows(r):
        return slice(r * rows_per, (r + 1) * rows_per)

    post_mix = ga_ref[...] * nw1_ref[...]
    pre_mlp = nw2_ref[...] * (1.0 + scm_ref[...])
    post_mlp = gm_ref[...] * nw3_ref[...]
    if emit_next:
        pre_next = nwn_ref[...] * (1.0 + scn_ref[...])

    def prologue(r):
        y = _dot(a_ref[rows(r), :], wo_ref[...])
        x1 = x_ref[rows(r), :] + _rms(y, post_mix)
        h = (_rms(x1, pre_mlp) + shm_ref[...]).astype(BF16)
        return x1, h

    def epilogue(r, x1, acc):
        x2 = x1 + _rms(acc, post_mlp)
        xo_ref[rows(r), :] = x2
        if emit_next:
            hn_ref[rows(r), :] = (_rms(x2, pre_next) + shn_ref[...]).astype(BF16)

    cur = prologue(0)
    done = None
    for r in range(splits):
        x1, h = cur
        acc = None
        for j in range(n_chunks):
            cols = slice(j * tf, (j + 1) * tf)
            u = jnp.square(jnp.maximum(_dot(h, w1_ref[:, cols]), 0.0)).astype(BF16)
            part = _dot(u, w2_ref[cols, :])
            acc = part if acc is None else acc + part
            if j == 0 and r + 1 < splits:
                cur = prologue(r + 1)
            if j == min(1, n_chunks - 1) and done is not None:
                epilogue(*done)
                done = None
        done = (r, x1, acc)
    epilogue(*done)


def _resident(shape):
    return pl.BlockSpec(shape, lambda *_: (0,) * len(shape), pipeline_mode=pl.Buffered(1))


def _mixer_mlp(a, w_out, x2, w1, w2, nw, mod, layer, seq, emit_next):
    t, d = x2.shape
    kd = a.shape[1]
    dff = w1.shape[1]
    tm, splits, tf = MLP_TM, MLP_SPLITS, MLP_TF
    per_b = seq // tm
    vec = lambda l, k: pl.BlockSpec((None, None, 1, d), lambda i: (l, i // per_b, 0, k))
    nvec = lambda l, k: pl.BlockSpec((None, None, 1, d), lambda i: (l, k, 0, 0))
    row = pl.BlockSpec((tm, d), lambda i: (i, 0))
    in_specs = [
        pl.BlockSpec((tm, kd), lambda i: (i, 0)), _resident((kd, d)), row,
        vec(layer, 2), nvec(layer, 1), nvec(layer, 2), vec(layer, 4), vec(layer, 3),
        _resident((d, dff)), _resident((dff, d)), vec(layer, 5), nvec(layer, 3),
    ]
    args = [a, w_out, x2, mod, nw, nw, mod, mod, w1, w2, mod, nw]
    out_specs = [row]
    out_shape = [jax.ShapeDtypeStruct((t, d), F32)]
    if emit_next:
        in_specs += [nvec(layer + 1, 0), vec(layer + 1, 1), vec(layer + 1, 0)]
        args += [nw, mod, mod]
        out_specs.append(row)
        out_shape.append(jax.ShapeDtypeStruct((t, d), BF16))
    return pl.pallas_call(
        functools.partial(_mixer_mlp_kernel, emit_next=emit_next, splits=splits, tf=tf),
        grid=(t // tm,),
        in_specs=in_specs,
        out_specs=out_specs,
        out_shape=out_shape,
        compiler_params=_cparams(("parallel",), MLP_VMEM_MIB),
        name="mixer_mlp_next" if emit_next else "mixer_mlp_last",
    )(*args)


def _mla_proj_kernel(h_ref, wc_ref, qn_ref, kvn_ref, wuq_ref, wukv_ref, pos_ref, inv_ref,
                     sign_ref, q_ref, k_ref, v_ref, *, heads, splits):
    q_lora = qn_ref.shape[1]
    kv_lora = kvn_ref.shape[1]
    nope_w = heads * MLA_NOPE
    rows_per = h_ref.shape[0] // splits
    lane = lax.broadcasted_iota(jnp.int32, (1, LANES), 1)
    first_half = lane < MLA_ROPE // 2
    pair = rows_per // 2

    def compress(r):
        rows = slice(r * rows_per, (r + 1) * rows_per)
        c_all = _dot(h_ref[rows, :], wc_ref[...])
        cq = _rms(c_all[:, :q_lora], qn_ref[...]).astype(BF16)
        ckv = _rms(c_all[:, q_lora:q_lora + kv_lora], kvn_ref[...]).astype(BF16)
        pos_a = pos_ref[r * rows_per:r * rows_per + pair, :]
        pos_b = pos_ref[r * rows_per + pair:(r + 1) * rows_per, :]
        ang = jnp.where(lane < MLA_ROPE, pos_a, pos_b).astype(F32) * inv_ref[...]
        cos2 = jnp.cos(ang)
        sin2 = jnp.sin(ang)
        cos = jnp.concatenate([cos2, pltpu.roll(cos2, MLA_ROPE, 1)], axis=0)
        sin = jnp.concatenate([sin2, pltpu.roll(sin2, MLA_ROPE, 1)], axis=0) * sign_ref[...]
        return cq, ckv, c_all[:, q_lora + kv_lora:], cos, sin

    cur = compress(0)
    for r in range(splits):
        rows = slice(r * rows_per, (r + 1) * rows_per)
        cq, ckv, kr_raw, cos, sin = cur

        def rotary(t, cos=cos, sin=sin):
            partner = jnp.where(first_half, pltpu.roll(t, LANES - MLA_ROPE // 2, 1),
                                pltpu.roll(t, MLA_ROPE // 2, 1))
            return t * cos + partner * sin

        kr = rotary(kr_raw).astype(BF16)
        for hd in range(heads):
            lo = hd * MLA_HEAD_PAD
            if hd % 2 == 0:
                kn = _dot(ckv, wukv_ref[:, hd * LANES:(hd + 2) * LANES])
                vv = _dot(ckv, wukv_ref[:, nope_w + hd * MLA_V:nope_w + (hd + 2) * MLA_V])
            if hd == 2 and r + 1 < splits:
                cur = compress(r + 1)
            qh = _dot(cq, wuq_ref[:, lo:lo + MLA_HEAD_PAD])
            q_ref[hd, rows, :LANES] = qh[:, :LANES].astype(BF16)
            q_ref[hd, rows, LANES:] = rotary(qh[:, LANES:]).astype(BF16)
            k_ref[hd, rows, :LANES] = kn[:, (hd % 2) * LANES:(hd % 2 + 1) * LANES].astype(BF16)
            k_ref[hd, rows, LANES:] = kr
            v_ref[hd, rows, :] = vv[:, (hd % 2) * MLA_V:(hd % 2 + 1) * MLA_V].astype(BF16)


def _mla_proj(h, wc, q_norm, kv_norm, wuq, wukv, pos, inv_m, sign_m, heads):
    t, d = h.shape
    tm, splits = 1024, 2
    qn = q_norm.reshape(1, -1) * float(MLA_DQK ** -0.5 * np.log2(np.e))
    kvn = kv_norm.reshape(1, -1)
    row = lambda w: pl.BlockSpec((tm, w), lambda i: (i, 0))
    return pl.pallas_call(
        functools.partial(_mla_proj_kernel, heads=heads, splits=splits),
        grid=(t // tm,),
        in_specs=[row(d), _resident(wc.shape), _resident(qn.shape), _resident(kvn.shape),
                  _resident(wuq.shape), _resident(wukv.shape), row(1), _resident((1, LANES)),
                  _resident((1, LANES))],
        out_specs=[pl.BlockSpec((heads, tm, w), lambda i: (0, i, 0))
                   for w in (MLA_HEAD_PAD, MLA_HEAD_PAD, MLA_V)],
        out_shape=[jax.ShapeDtypeStruct((heads, t, w), BF16)
                   for w in (MLA_HEAD_PAD, MLA_HEAD_PAD, MLA_V)],
        compiler_params=_cparams(("parallel",), 48),
        name="mla_proj",
    )(h, wc, qn, kvn, wuq, wukv, pos, inv_m, sign_m)


def _attn_kernel(q_ref, k_ref, v_ref, o_ref, v1_scr, *, sub, heads_per_step):
    seq = q_ref.shape[1]
    n_sub = seq // sub
    for hh in range(heads_per_step):
        v1_scr[hh, :, :MLA_V] = v_ref[hh]
        v1_scr[hh, :, MLA_V:] = jnp.ones((seq, MLA_V), BF16)

    def scores(i):
        hh, r = divmod(i, n_sub)
        return _dot_nt(q_ref[hh, r * sub:(r + 1) * sub, :], k_ref[hh])

    s_next = scores(0)
    for i in range(heads_per_step * n_sub):
        hh, r = divmod(i, n_sub)
        s = s_next
        if i + 1 < heads_per_step * n_sub:
            s_next = scores(i + 1)
        p = jnp.exp2(s - jnp.max(s, axis=-1, keepdims=True))
        ov = _dot(p.astype(BF16), v1_scr[hh])
        o_ref[r * sub:(r + 1) * sub, hh * MLA_V:(hh + 1) * MLA_V] = (
            ov[:, :MLA_V] / ov[:, MLA_V:MLA_V + 1]).astype(BF16)


def _attention(q, k, v, batch, seq, heads):
    t = q.shape[1]
    sub, hp = 256, ATTN_HEADS_PER_STEP
    slab = lambda w: pl.BlockSpec((hp, seq, w), lambda b, h: (h, b, 0))
    return pl.pallas_call(
        functools.partial(_attn_kernel, sub=sub, heads_per_step=hp),
        grid=(batch, heads // hp),
        in_specs=[slab(MLA_HEAD_PAD), slab(MLA_HEAD_PAD), slab(MLA_V)],
        out_specs=pl.BlockSpec((seq, hp * MLA_V), lambda b, h: (b, h)),
        out_shape=jax.ShapeDtypeStruct((t, heads * MLA_V), BF16),
        scratch_shapes=[pltpu.VMEM((hp, seq, 2 * MLA_V), BF16)],
        compiler_params=_cparams(("parallel", "parallel"), 48),
        name="mla_attention",
    )(q, k, v)


def _mla_weights(w_in, w_uq, w_ukv, heads):
    q_lora = w_uq.shape[0]
    kv_lora = w_ukv.shape[0]
    wc = jnp.pad(w_in, [(0, 0), (0, LANES - MLA_ROPE)])
    uq = w_uq.reshape(q_lora, heads, MLA_DQK)
    wuq = jnp.pad(uq, [(0, 0), (0, 0), (0, MLA_HEAD_PAD - MLA_DQK)]).reshape(q_lora, heads * MLA_HEAD_PAD)
    ukv = w_ukv.reshape(kv_lora, heads, MLA_NOPE + MLA_V)
    wukv = jnp.concatenate([
        ukv[..., :MLA_NOPE].reshape(kv_lora, heads * MLA_NOPE),
        ukv[..., MLA_NOPE:].reshape(kv_lora, heads * MLA_V),
    ], axis=-1)
    return [w.astype(BF16) for w in (wc, wuq, wukv)]


def kernel(x, c, positions, norm_w, ada_w, ada_b, ret_w_in, ret_decay_logit_fwd,
           ret_decay_logit_bwd, ret_gn_w, ret_w_out, mla_w_in, mla_q_norm, mla_w_uq,
           mla_kv_norm, mla_w_ukv, mla_w_out, mlp_w1, mlp_w2):
    batch, seq, d = x.shape
    depth = norm_w.shape[0]
    assert depth == 2
    t = batch * seq
    ret_heads = ret_decay_logit_fwd.shape[1]
    mla_heads = mla_w_out.shape[1] // MLA_V

    mod = _ada_mod(c, ada_w, ada_b).reshape(depth, batch, 1, N_ADA * d)
    nw = norm_w.reshape(depth, 4, 1, d)
    inv_r, inv_m, sign_m = _rope_freqs()
    pos = positions.reshape(t, 1)
    x2 = x.reshape(t, d)

    qk, vg = _ret_proj(x2, nw, mod, 0, ret_w_in[0].astype(BF16), pos, inv_r,
                       ret_decay_logit_fwd[0], ret_decay_logit_bwd[0], seq, ret_heads)
    y = _retention(qk, vg, ret_decay_logit_fwd[0], ret_decay_logit_bwd[0], ret_gn_w[0],
                   batch, seq, ret_heads)
    x2, h = _mixer_mlp(y, ret_w_out[0].astype(BF16), x2, mlp_w1[0].astype(BF16),
                       mlp_w2[0].astype(BF16), nw, mod, 0, seq, True)

    wc, wuq, wukv = _mla_weights(mla_w_in[0], mla_w_uq[0], mla_w_ukv[0], mla_heads)
    q, k, v = _mla_proj(h, wc, mla_q_norm[0], mla_kv_norm[0], wuq, wukv, pos, inv_m, sign_m,
                        mla_heads)
    o = _attention(q, k, v, batch, seq, mla_heads)
    (x2,) = _mixer_mlp(o, mla_w_out[0].astype(BF16), x2, mlp_w1[1].astype(BF16),
                       mlp_w2[1].astype(BF16), nw, mod, 1, seq, False)
    return x2.reshape(batch, seq, d)
```

```python
import functools

import numpy as np
import jax
import jax.numpy as jnp
from jax import lax
from jax.experimental import pallas as pl
from jax.experimental.pallas import tpu as pltpu

F32 = jnp.float32
BF16 = jnp.bfloat16

RMS_EPS = 1e-6
GN_EPS = 1e-5
ROPE_THETA = 10000.0
N_ADA = 6

RET_DK = 256
RET_DV = 512
RET_CHUNK = 256

MLA_NOPE = 128
MLA_ROPE = 64
MLA_V = 128
MLA_DQK = MLA_NOPE + MLA_ROPE
MLA_HEAD_PAD = 256

LANES = 128
MIB = 1024 * 1024

ADA_TN, ADA_VMEM_MIB = 1024, 32
RET_PROJ_TM, RET_PROJ_GROUPS, RET_PROJ_TN, RET_PROJ_VMEM_MIB = 512, 2, 1024, 56
RETENTION_VMEM_MIB = 48
MLP_TF = 1024
MLP_NEXT_TM, MLP_NEXT_GROUPS, MLP_NEXT_VMEM_MIB = 512, 2, 56
MLP_LAST_TM, MLP_LAST_GROUPS, MLP_LAST_VMEM_MIB = 1024, 4, 58
MLA_PROJ_TM, MLA_PROJ_GROUPS, MLA_PROJ_VMEM_MIB = 1024, 2, 48
ATTN_SUB, ATTN_HEADS_PER_STEP, ATTN_VMEM_MIB = 256, 2, 48


def _cparams(semantics, vmem_mib):
    return pltpu.CompilerParams(dimension_semantics=semantics,
                                vmem_limit_bytes=vmem_mib * MIB)


def _rms(t, w):
    return t * lax.rsqrt(jnp.mean(t * t, axis=-1, keepdims=True) + RMS_EPS) * w


def _dot(a, b):
    return jnp.dot(a, b, preferred_element_type=F32)


def _dot_nt(a, b):
    return lax.dot_general(a, b, (((1,), (1,)), ((), ())), preferred_element_type=F32)


def _dot_tn(a, b):
    return lax.dot_general(a, b, (((0,), (0,)), ((), ())), preferred_element_type=F32)


def _ada_kernel(c_ref, w_ref, b_ref, o_ref):
    c_act = jax.nn.silu(c_ref[...])
    o_ref[...] = _dot(c_act.astype(BF16), w_ref[...].astype(BF16)) + b_ref[...]


def _ada_mod(c, ada_w, ada_b):
    depth, d, n = ada_w.shape
    b = c.shape[0]
    tn = ADA_TN
    return pl.pallas_call(
        _ada_kernel,
        grid=(depth, n // tn),
        in_specs=[
            pl.BlockSpec((b, d), lambda l, j: (0, 0)),
            pl.BlockSpec((None, d, tn), lambda l, j: (l, 0, j)),
            pl.BlockSpec((None, 1, tn), lambda l, j: (l, 0, j)),
        ],
        out_specs=pl.BlockSpec((None, b, tn), lambda l, j: (l, 0, j)),
        out_shape=jax.ShapeDtypeStruct((depth, b, n), F32),
        compiler_params=_cparams(("arbitrary", "arbitrary"), ADA_VMEM_MIB),
        name="ada_mod",
    )(c, ada_w, ada_b.reshape(depth, 1, n))


def _rope_freqs():
    half_m = MLA_ROPE // 2
    assert RET_DK // 2 == LANES
    inv_r = ROPE_THETA ** (-jnp.arange(0, RET_DK, 2, dtype=F32) / RET_DK)
    inv_m = ROPE_THETA ** (-jnp.arange(0, MLA_ROPE, 2, dtype=F32) / MLA_ROPE)
    inv_m = jnp.tile(inv_m, LANES // half_m)
    sign_m = jnp.concatenate([-jnp.ones((half_m,), F32), jnp.ones((half_m,), F32),
                              jnp.zeros((LANES - MLA_ROPE,), F32)])
    return inv_r.reshape(1, LANES), inv_m.reshape(1, LANES), sign_m.reshape(1, LANES)


def _ret_proj_kernel(x_ref, nw_ref, sc_ref, sh_ref, w_ref, pos_ref, inv_ref, lf_ref, lb_ref,
                     qk_ref, vg_ref, *, heads, k_scale, splits, tn):
    rows_per = x_ref.shape[0] // splits
    hk, hv = heads * RET_DK, heads * RET_DV
    half = RET_DK // 2
    local = lax.broadcasted_iota(jnp.int32, (rows_per, 1), 0).astype(F32)
    sdec_f = jnp.exp((rows_per - 1.0 - local) * jax.nn.log_sigmoid(lf_ref[...]))
    sdec_b = jnp.exp(local * jax.nn.log_sigmoid(lb_ref[...]))

    def rows(r):
        return slice(r * rows_per, (r + 1) * rows_per)

    pre_gain = nw_ref[...] * (1.0 + sc_ref[...])

    def prologue(r):
        h = (_rms(x_ref[rows(r), :], pre_gain) + sh_ref[...]).astype(BF16)
        ang = pos_ref[rows(r), :].astype(F32) * inv_ref[...]
        return h, jnp.cos(ang), jnp.sin(ang)

    def emit(r, col, p, cos, sin):
        if col >= 2 * hk:
            is_g = col >= 2 * hk + hv
            for g in range(tn // RET_DV):
                hd = (col - 2 * hk - is_g * hv) // RET_DV + g
                blk = p[:, g * RET_DV:(g + 1) * RET_DV]
                vg_ref[is_g * heads + hd, rows(r), :] = (jax.nn.silu(blk) if is_g else blk).astype(BF16)
        else:
            is_k = col >= hk
            if is_k:
                cos, sin = cos * k_scale, sin * k_scale
            for g in range(tn // RET_DK):
                hd = (col - is_k * hk) // RET_DK + g
                t1 = p[:, g * RET_DK:g * RET_DK + half]
                t2 = p[:, g * RET_DK + half:(g + 1) * RET_DK]
                r1 = t1 * cos - t2 * sin
                r2 = t1 * sin + t2 * cos
                qk_ref[is_k * heads + hd, rows(r), :half] = r1.astype(BF16)
                qk_ref[is_k * heads + hd, rows(r), half:] = r2.astype(BF16)
                if is_k:
                    for slab, dec in ((2 * heads + hd, sdec_f), (3 * heads + hd, sdec_b)):
                        d = dec[:, hd:hd + 1]
                        qk_ref[slab, rows(r), :half] = (r1 * d).astype(BF16)
                        qk_ref[slab, rows(r), half:] = (r2 * d).astype(BF16)

    order = ([c for c in range(0, 2 * hk, tn)] + [c for c in range(2 * hk + hv, 2 * hk + 2 * hv, tn)]
             + [c for c in range(2 * hk, 2 * hk + hv, tn)])
    cur = prologue(0)
    for r in range(splits):
        h, cos, sin = cur
        for idx, col in enumerate(order):
            p = _dot(h, w_ref[:, col:col + tn])
            if idx == 0 and r + 1 < splits:
                cur = prologue(r + 1)
            emit(r, col, p, cos, sin)


def _ret_proj(x2, nw, mod, layer, w_in, pos, inv_r, logit_f, logit_b, seq, heads):
    t, d = x2.shape
    n = w_in.shape[1]
    hk, hv = heads * RET_DK, heads * RET_DV
    tm, splits, tn = RET_PROJ_TM, RET_PROJ_GROUPS, RET_PROJ_TN
    assert hk % tn == 0 and hv % tn == 0 and tm // splits == RET_CHUNK and seq % tm == 0
    per_b = seq // tm
    vec = lambda k: pl.BlockSpec((None, None, 1, d), lambda i: (layer, i // per_b, 0, k))
    return pl.pallas_call(
        functools.partial(_ret_proj_kernel, heads=heads, k_scale=RET_DK ** -0.5, splits=splits, tn=tn),
        grid=(t // tm,),
        in_specs=[
            pl.BlockSpec((tm, d), lambda i: (i, 0)),
            pl.BlockSpec((None, None, 1, d), lambda i: (layer, 0, 0, 0)),
            vec(1), vec(0),
            _resident((d, n)),
            pl.BlockSpec((tm, 1), lambda i: (i, 0)),
            _resident((1, LANES)),
            _resident((1, heads)), _resident((1, heads)),
        ],
        out_specs=[pl.BlockSpec((4 * heads, tm, RET_DK), lambda i: (0, i, 0)),
                   pl.BlockSpec((2 * heads, tm, RET_DV), lambda i: (0, i, 0))],
        out_shape=[jax.ShapeDtypeStruct((4 * heads, t, RET_DK), BF16),
                   jax.ShapeDtypeStruct((2 * heads, t, RET_DV), BF16)],
        compiler_params=_cparams(("parallel",), RET_PROJ_VMEM_MIB),
        name="ret_proj",
    )(x2, nw, mod, mod, w_in, pos, inv_r, logit_f.reshape(1, heads), logit_b.reshape(1, heads))


def _retention_kernel(lf_ref, lb_ref, q_ref, k_ref, kf_ref, kb_ref, v_ref, sg_ref, gn_ref, o_ref,
                      sf_all, sb_all):
    c = RET_CHUNK
    n = q_ref.shape[0] // c
    lg_f = jax.nn.log_sigmoid(lf_ref[...].astype(F32))
    lg_b = jax.nn.log_sigmoid(lb_ref[...].astype(F32))
    row = lax.broadcasted_iota(jnp.int32, (c, c), 0).astype(F32)
    col = lax.broadcasted_iota(jnp.int32, (c, c), 1).astype(F32)
    pos = lax.broadcasted_iota(jnp.int32, (c, 1), 0).astype(F32)
    diff = row - col
    mask_f = diff >= 0
    mask_b = diff < 0
    decay = jnp.where(mask_f, jnp.exp(jnp.where(mask_f, diff * lg_f, 0.0)),
                      jnp.exp(jnp.where(mask_b, -diff * lg_b, 0.0)))
    cross_f = jnp.exp((pos + 1.0) * lg_f)
    cross_b = jnp.exp((c - pos) * lg_b)
    cdec_f = jnp.exp(c * lg_f)
    cdec_b = jnp.exp(c * lg_b)
    gn_w = gn_ref[...]

    def rows(ci):
        return slice(ci * c, (ci + 1) * c)

    sf = jnp.zeros(sf_all.shape[1:], F32)
    sb = jnp.zeros(sb_all.shape[1:], F32)
    for i in range(n):
        cb = n - 1 - i
        sf_all[i] = sf.astype(BF16)
        sb_all[cb] = sb.astype(BF16)
        if i + 1 < n:
            sf = sf * cdec_f + _dot_tn(kf_ref[rows(i), :], v_ref[rows(i), :])
            sb = sb * cdec_b + _dot_tn(kb_ref[rows(cb), :], v_ref[rows(cb), :])

    for ci in range(n):
        r = rows(ci)
        qi, vi = q_ref[r, :], v_ref[r, :]
        s = _dot_nt(qi, k_ref[r, :]) * decay
        y = (_dot(s.astype(BF16), vi) + _dot(qi, sf_all[ci]) * cross_f
             + _dot(qi, sb_all[ci]) * cross_b)
        mu = jnp.mean(y, axis=-1, keepdims=True)
        yc = y - mu
        var = jnp.mean(yc * yc, axis=-1, keepdims=True)
        yn = yc * lax.rsqrt(var + GN_EPS) * gn_w
        o_ref[r, :] = (sg_ref[r, :].astype(F32) * yn).astype(BF16)


def _retention(qk, vg, logit_f, logit_b, gn_w, batch, seq, heads):
    t = qk.shape[1]
    hv = heads * RET_DV
    scalar = pl.BlockSpec((None, 1, 1), lambda b, h: (h, 0, 0))
    slab = lambda m, w: pl.BlockSpec((None, seq, w), lambda b, h: (m * heads + h, b, 0))
    return pl.pallas_call(
        _retention_kernel,
        grid=(batch, heads),
        in_specs=[
            scalar, scalar,
            *[slab(m, RET_DK) for m in range(4)],
            slab(0, RET_DV), slab(1, RET_DV),
            pl.BlockSpec((1, RET_DV), lambda b, h: (0, h)),
        ],
        out_specs=pl.BlockSpec((seq, RET_DV), lambda b, h: (b, h)),
        out_shape=jax.ShapeDtypeStruct((t, hv), BF16),
        scratch_shapes=[
            pltpu.VMEM((seq // RET_CHUNK, RET_DK, RET_DV), BF16),
            pltpu.VMEM((seq // RET_CHUNK, RET_DK, RET_DV), BF16),
        ],
        compiler_params=_cparams(("parallel", "parallel"), RETENTION_VMEM_MIB),
        name="retention",
    )(logit_f.reshape(heads, 1, 1), logit_b.reshape(heads, 1, 1), *([qk] * 4), vg, vg,
      gn_w.reshape(1, hv))


def _mixer_mlp_kernel(*refs, emit_next, splits, tf):
    if emit_next:
        (a_ref, wo_ref, x_ref, ga_ref, nw1_ref, nw2_ref, scm_ref, shm_ref, w1_ref, w2_ref,
         gm_ref, nw3_ref, nwn_ref, scn_ref, shn_ref, xo_ref, hn_ref) = refs
    else:
        (a_ref, wo_ref, x_ref, ga_ref, nw1_ref, nw2_ref, scm_ref, shm_ref, w1_ref, w2_ref,
         gm_ref, nw3_ref, xo_ref) = refs
    rows_per = x_ref.shape[0] // splits
    n_chunks = w1_ref.shape[1] // tf

    def rows(r):
        return slice(r * rows_per, (r + 1) * rows_per)

    post_mix = ga_ref[...] * nw1_ref[...]
    pre_mlp = nw2_ref[...] * (1.0 + scm_ref[...])
    post_mlp = gm_ref[...] * nw3_ref[...]
    if emit_next:
        pre_next = nwn_ref[...] * (1.0 + scn_ref[...])

    def prologue(r):
        y = _dot(a_ref[rows(r), :], wo_ref[...])
        x1 = x_ref[rows(r), :] + _rms(y, post_mix)
        h = (_rms(x1, pre_mlp) + shm_ref[...]).astype(BF16)
        return x1, h

    def epilogue(r, x1, acc):
        x2 = x1 + _rms(acc, post_mlp)
        xo_ref[rows(r), :] = x2
        if emit_next:
            hn_ref[rows(r), :] = (_rms(x2, pre_next) + shn_ref[...]).astype(BF16)

    cur = prologue(0)
    done = None
    for r in range(splits):
        x1, h = cur
        acc = None
        for j in range(n_chunks):
            cols = slice(j * tf, (j + 1) * tf)
            u = jnp.square(jnp.maximum(_dot(h, w1_ref[:, cols]), 0.0)).astype(BF16)
            part = _dot(u, w2_ref[cols, :])
            acc = part if acc is None else acc + part
            if j == 0 and r + 1 < splits:
                cur = prologue(r + 1)
            if j == min(1, n_chunks - 1) and done is not None:
                epilogue(*done)
                done = None
        done = (r, x1, acc)
    epilogue(*done)


def _resident(shape):
    return pl.BlockSpec(shape, lambda *_: (0,) * len(shape), pipeline_mode=pl.Buffered(1))


def _mixer_mlp(a, w_out, x2, w1, w2, nw, mod, layer, seq, emit_next):
    t, d = x2.shape
    kd = a.shape[1]
    dff = w1.shape[1]
    tf = MLP_TF
    if emit_next:
        tm, splits, vmem_mib = MLP_NEXT_TM, MLP_NEXT_GROUPS, MLP_NEXT_VMEM_MIB
    else:
        tm, splits, vmem_mib = MLP_LAST_TM, MLP_LAST_GROUPS, MLP_LAST_VMEM_MIB
    per_b = seq // tm
    vec = lambda l, k: pl.BlockSpec((None, None, 1, d), lambda i: (l, i // per_b, 0, k))
    nvec = lambda l, k: pl.BlockSpec((None, None, 1, d), lambda i: (l, k, 0, 0))
    row = pl.BlockSpec((tm, d), lambda i: (i, 0))
    in_specs = [
        pl.BlockSpec((tm, kd), lambda i: (i, 0)), _resident((kd, d)), row,
        vec(layer, 2), nvec(layer, 1), nvec(layer, 2), vec(layer, 4), vec(layer, 3),
        _resident((d, dff)), _resident((dff, d)), vec(layer, 5), nvec(layer, 3),
    ]
    args = [a, w_out, x2, mod, nw, nw, mod, mod, w1, w2, mod, nw]
    out_specs = [row]
    out_shape = [jax.ShapeDtypeStruct((t, d), F32)]
    if emit_next:
        in_specs += [nvec(layer + 1, 0), vec(layer + 1, 1), vec(layer + 1, 0)]
        args += [nw, mod, mod]
        out_specs.append(row)
        out_shape.append(jax.ShapeDtypeStruct((t, d), BF16))
    return pl.pallas_call(
        functools.partial(_mixer_mlp_kernel, emit_next=emit_next, splits=splits, tf=tf),
        grid=(t // tm,),
        in_specs=in_specs,
        out_specs=out_specs,
        out_shape=out_shape,
        compiler_params=_cparams(("parallel",), vmem_mib),
        name="mixer_mlp_next" if emit_next else "mixer_mlp_last",
    )(*args)


def _mla_proj_kernel(h_ref, wc_ref, qn_ref, kvn_ref, wuq_ref, wukv_ref, pos_ref, inv_ref,
                     sign_ref, q_ref, k_ref, v_ref, *, heads, splits):
    q_lora = qn_ref.shape[1]
    kv_lora = kvn_ref.shape[1]
    nope_w = heads * MLA_NOPE
    rows_per = h_ref.shape[0] // splits
    lane = lax.broadcasted_iota(jnp.int32, (1, LANES), 1)
    first_half = lane < MLA_ROPE // 2
    pair = rows_per // 2

    def compress(r):
        rows = slice(r * rows_per, (r + 1) * rows_per)
        c_all = _dot(h_ref[rows, :], wc_ref[...])
        cq = _rms(c_all[:, :q_lora], qn_ref[...]).astype(BF16)
        ckv = _rms(c_all[:, q_lora:q_lora + kv_lora], kvn_ref[...]).astype(BF16)
        pos_a = pos_ref[r * rows_per:r * rows_per + pair, :]
        pos_b = pos_ref[r * rows_per + pair:(r + 1) * rows_per, :]
        ang = jnp.where(lane < MLA_ROPE, pos_a, pos_b).astype(F32) * inv_ref[...]
        cos2 = jnp.cos(ang)
        sin2 = jnp.sin(ang)
        cos = jnp.concatenate([cos2, pltpu.roll(cos2, MLA_ROPE, 1)], axis=0)
        sin = jnp.concatenate([sin2, pltpu.roll(sin2, MLA_ROPE, 1)], axis=0) * sign_ref[...]
        return cq, ckv, c_all[:, q_lora + kv_lora:], cos, sin

    cur = compress(0)
    for r in range(splits):
        rows = slice(r * rows_per, (r + 1) * rows_per)
        cq, ckv, kr_raw, cos, sin = cur

        def rotary(t, cos=cos, sin=sin):
            partner = jnp.where(first_half, pltpu.roll(t, LANES - MLA_ROPE // 2, 1),
                                pltpu.roll(t, MLA_ROPE // 2, 1))
            return t * cos + partner * sin

        kr = rotary(kr_raw).astype(BF16)
        for hd in range(heads):
            lo = hd * MLA_HEAD_PAD
            if hd % 2 == 0:
                kn = _dot(ckv, wukv_ref[:, hd * LANES:(hd + 2) * LANES])
                vv = _dot(ckv, wukv_ref[:, nope_w + hd * MLA_V:nope_w + (hd + 2) * MLA_V])
            if hd == 2 and r + 1 < splits:
                cur = compress(r + 1)
            qh = _dot(cq, wuq_ref[:, lo:lo + MLA_HEAD_PAD])
            q_ref[hd, rows, :LANES] = qh[:, :LANES].astype(BF16)
            q_ref[hd, rows, LANES:] = rotary(qh[:, LANES:]).astype(BF16)
            k_ref[hd, rows, :LANES] = kn[:, (hd % 2) * LANES:(hd % 2 + 1) * LANES].astype(BF16)
            k_ref[hd, rows, LANES:] = kr
            v_ref[hd, rows, :] = vv[:, (hd % 2) * MLA_V:(hd % 2 + 1) * MLA_V].astype(BF16)


def _mla_proj(h, wc, q_norm, kv_norm, wuq, wukv, pos, inv_m, sign_m, heads):
    t, d = h.shape
    tm, splits = MLA_PROJ_TM, MLA_PROJ_GROUPS
    qn = q_norm.reshape(1, -1) * float(MLA_DQK ** -0.5 * np.log2(np.e))
    kvn = kv_norm.reshape(1, -1)
    row = lambda w: pl.BlockSpec((tm, w), lambda i: (i, 0))
    return pl.pallas_call(
        functools.partial(_mla_proj_kernel, heads=heads, splits=splits),
        grid=(t // tm,),
        in_specs=[row(d), _resident(wc.shape), _resident(qn.shape), _resident(kvn.shape),
                  _resident(wuq.shape), _resident(wukv.shape), row(1), _resident((1, LANES)),
                  _resident((1, LANES))],
        out_specs=[pl.BlockSpec((heads, tm, w), lambda i: (0, i, 0))
                   for w in (MLA_HEAD_PAD, MLA_HEAD_PAD, MLA_V)],
        out_shape=[jax.ShapeDtypeStruct((heads, t, w), BF16)
                   for w in (MLA_HEAD_PAD, MLA_HEAD_PAD, MLA_V)],
        compiler_params=_cparams(("parallel",), MLA_PROJ_VMEM_MIB),
        name="mla_proj",
    )(h, wc, qn, kvn, wuq, wukv, pos, inv_m, sign_m)


def _attn_kernel(q_ref, k_ref, v_ref, o_ref, v1_scr, *, sub, heads_per_step):
    seq = q_ref.shape[1]
    n_sub = seq // sub
    for hh in range(heads_per_step):
        v1_scr[hh, :, :MLA_V] = v_ref[hh]
        v1_scr[hh, :, MLA_V:] = jnp.ones((seq, MLA_V), BF16)

    def scores(i):
        hh, r = divmod(i, n_sub)
        return _dot_nt(q_ref[hh, r * sub:(r + 1) * sub, :], k_ref[hh])

    s_next = scores(0)
    for i in range(heads_per_step * n_sub):
        hh, r = divmod(i, n_sub)
        s = s_next
        if i + 1 < heads_per_step * n_sub:
            s_next = scores(i + 1)
        p = jnp.exp2(s - jnp.max(s, axis=-1, keepdims=True))
        ov = _dot(p.astype(BF16), v1_scr[hh])
        o_ref[r * sub:(r + 1) * sub, hh * MLA_V:(hh + 1) * MLA_V] = (
            ov[:, :MLA_V] / ov[:, MLA_V:MLA_V + 1]).astype(BF16)


def _attention(q, k, v, batch, seq, heads):
    t = q.shape[1]
    sub, hp = ATTN_SUB, ATTN_HEADS_PER_STEP
    slab = lambda w: pl.BlockSpec((hp, seq, w), lambda b, h: (h, b, 0))
    return pl.pallas_call(
        functools.partial(_attn_kernel, sub=sub, heads_per_step=hp),
        grid=(batch, heads // hp),
        in_specs=[slab(MLA_HEAD_PAD), slab(MLA_HEAD_PAD), slab(MLA_V)],
        out_specs=pl.BlockSpec((seq, hp * MLA_V), lambda b, h: (b, h)),
        out_shape=jax.ShapeDtypeStruct((t, heads * MLA_V), BF16),
        scratch_shapes=[pltpu.VMEM((hp, seq, 2 * MLA_V), BF16)],
        compiler_params=_cparams(("parallel", "parallel"), ATTN_VMEM_MIB),
        name="mla_attention",
    )(q, k, v)


def _mla_weights(w_in, w_uq, w_ukv, heads):
    q_lora = w_uq.shape[0]
    kv_lora = w_ukv.shape[0]
    wc = jnp.pad(w_in, [(0, 0), (0, LANES - MLA_ROPE)])
    uq = w_uq.reshape(q_lora, heads, MLA_DQK)
    wuq = jnp.pad(uq, [(0, 0), (0, 0), (0, MLA_HEAD_PAD - MLA_DQK)]).reshape(q_lora, heads * MLA_HEAD_PAD)
    ukv = w_ukv.reshape(kv_lora, heads, MLA_NOPE + MLA_V)
    wukv = jnp.concatenate([
        ukv[..., :MLA_NOPE].reshape(kv_lora, heads * MLA_NOPE),
        ukv[..., MLA_NOPE:].reshape(kv_lora, heads * MLA_V),
    ], axis=-1)
    return [w.astype(BF16) for w in (wc, wuq, wukv)]


def kernel(x, c, positions, norm_w, ada_w, ada_b, ret_w_in, ret_decay_logit_fwd,
           ret_decay_logit_bwd, ret_gn_w, ret_w_out, mla_w_in, mla_q_norm, mla_w_uq,
           mla_kv_norm, mla_w_ukv, mla_w_out, mlp_w1, mlp_w2):
    batch, seq, d = x.shape
    depth = norm_w.shape[0]
    assert depth == 2
    t = batch * seq
    ret_heads = ret_decay_logit_fwd.shape[1]
    mla_heads = mla_w_out.shape[1] // MLA_V

    mod = _ada_mod(c, ada_w, ada_b).reshape(depth, batch, 1, N_ADA * d)
    nw = norm_w.reshape(depth, 4, 1, d)
    inv_r, inv_m, sign_m = _rope_freqs()
    pos = positions.reshape(t, 1)
    x2 = x.reshape(t, d)

    qk, vg = _ret_proj(x2, nw, mod, 0, ret_w_in[0].astype(BF16), pos, inv_r,
                       ret_decay_logit_fwd[0], ret_decay_logit_bwd[0], seq, ret_heads)
    y = _retention(qk, vg, ret_decay_logit_fwd[0], ret_decay_logit_bwd[0], ret_gn_w[0],
                   batch, seq, ret_heads)
    x2, h = _mixer_mlp(y, ret_w_out[0].astype(BF16), x2, mlp_w1[0].astype(BF16),
                       mlp_w2[0].astype(BF16), nw, mod, 0, seq, True)

    wc, wuq, wukv = _mla_weights(mla_w_in[0], mla_w_uq[0], mla_w_ukv[0], mla_heads)
    q, k, v = _mla_proj(h, wc, mla_q_norm[0], mla_kv_norm[0], wuq, wukv, pos, inv_m, sign_m,
                        mla_heads)
    o = _attention(q, k, v, batch, seq, mla_heads)
    (x2,) = _mixer_mlp(o, mla_w_out[0].astype(BF16), x2, mlp_w1[1].astype(BF16),
                       mlp_w2[1].astype(BF16), nw, mod, 1, seq, False)
    return x2.reshape(batch, seq, d)
```

```python
import functools

import numpy as np
import jax
import jax.numpy as jnp
from jax import lax
from jax.experimental import pallas as pl
from jax.experimental.pallas import tpu as pltpu

F32 = jnp.float32
BF16 = jnp.bfloat16

RMS_EPS = 1e-6
GN_EPS = 1e-5
ROPE_THETA = 10000.0
N_ADA = 6

RET_DK = 256
RET_DV = 512
RET_CHUNK = 256

MLA_NOPE = 128
MLA_ROPE = 64
MLA_V = 128
MLA_DQK = MLA_NOPE + MLA_ROPE
MLA_HEAD_PAD = 256

LANES = 128
MIB = 1024 * 1024

ADA_TN, ADA_VMEM_MIB = 1024, 32
RET_PROJ_TM, RET_PROJ_GROUPS, RET_PROJ_TN, RET_PROJ_VMEM_MIB = 512, 2, 1024, 56
RETENTION_VMEM_MIB = 48
MLP_TF = 1024
MLP_NEXT_TM, MLP_NEXT_GROUPS, MLP_NEXT_VMEM_MIB = 512, 2, 56
MLP_LAST_TM, MLP_LAST_GROUPS, MLP_LAST_VMEM_MIB = 1024, 4, 58
MLA_PROJ_TM, MLA_PROJ_GROUPS, MLA_PROJ_VMEM_MIB = 1024, 2, 48
ATTN_SUB, ATTN_HEADS_PER_STEP, ATTN_VMEM_MIB = 256, 2, 48
WEIGHT_STAGE = (512, 1024)


def _cparams(semantics, vmem_mib):
    return pltpu.CompilerParams(dimension_semantics=semantics,
                                vmem_limit_bytes=vmem_mib * MIB)


def _rms(t, w):
    return t * lax.rsqrt(jnp.mean(t * t, axis=-1, keepdims=True) + RMS_EPS) * w


def _dot(a, b):
    return jnp.dot(a, b, preferred_element_type=F32)


def _dot_nt(a, b):
    return lax.dot_general(a, b, (((1,), (1,)), ((), ())), preferred_element_type=F32)


def _dot_tn(a, b):
    return lax.dot_general(a, b, (((0,), (0,)), ((), ())), preferred_element_type=F32)


def _ada_kernel(c_ref, w_ref, b_ref, o_ref):
    c_act = jax.nn.silu(c_ref[...])
    o_ref[...] = _dot(c_act.astype(BF16), w_ref[...].astype(BF16)) + b_ref[...]


def _ada_mod(c, ada_w, ada_b):
    depth, d, n = ada_w.shape
    b = c.shape[0]
    tn = ADA_TN
    return pl.pallas_call(
        _ada_kernel,
        grid=(depth, n // tn),
        in_specs=[
            pl.BlockSpec((b, d), lambda l, j: (0, 0)),
            pl.BlockSpec((None, d, tn), lambda l, j: (l, 0, j)),
            pl.BlockSpec((None, 1, tn), lambda l, j: (l, 0, j)),
        ],
        out_specs=pl.BlockSpec((None, b, tn), lambda l, j: (l, 0, j)),
        out_shape=jax.ShapeDtypeStruct((depth, b, n), F32),
        compiler_params=_cparams(("arbitrary", "arbitrary"), ADA_VMEM_MIB),
        name="ada_mod",
    )(c, ada_w, ada_b.reshape(depth, 1, n))


def _rope_freqs():
    half_m = MLA_ROPE // 2
    assert RET_DK // 2 == LANES
    inv_r = ROPE_THETA ** (-jnp.arange(0, RET_DK, 2, dtype=F32) / RET_DK)
    inv_m = ROPE_THETA ** (-jnp.arange(0, MLA_ROPE, 2, dtype=F32) / MLA_ROPE)
    inv_m = jnp.tile(inv_m, LANES // half_m)
    sign_m = jnp.concatenate([-jnp.ones((half_m,), F32), jnp.ones((half_m,), F32),
                              jnp.zeros((LANES - MLA_ROPE,), F32)])
    return inv_r.reshape(1, LANES), inv_m.reshape(1, LANES), sign_m.reshape(1, LANES)


def _ret_proj_kernel(x_ref, nw_ref, sc_ref, sh_ref, w_hbm, pos_ref, inv_ref, lf_ref, lb_ref,
                     qk_ref, vg_ref, w_ref, stage, *, heads, k_scale, splits, tn):
    @pl.when(pl.program_id(0) == 0)
    def _():
        _stage_weight(w_hbm.at[0], w_ref, stage)

    rows_per = x_ref.shape[0] // splits
    hk, hv = heads * RET_DK, heads * RET_DV
    half = RET_DK // 2
    local = lax.broadcasted_iota(jnp.int32, (rows_per, 1), 0).astype(F32)
    sdec_f = jnp.exp((rows_per - 1.0 - local) * jax.nn.log_sigmoid(lf_ref[...]))
    sdec_b = jnp.exp(local * jax.nn.log_sigmoid(lb_ref[...]))

    def rows(r):
        return slice(r * rows_per, (r + 1) * rows_per)

    pre_gain = nw_ref[...] * (1.0 + sc_ref[...])

    def prologue(r):
        h = (_rms(x_ref[rows(r), :], pre_gain) + sh_ref[...]).astype(BF16)
        ang = pos_ref[rows(r), :].astype(F32) * inv_ref[...]
        return h, jnp.cos(ang), jnp.sin(ang)

    def emit(r, col, p, cos, sin):
        if col >= 2 * hk:
            is_g = col >= 2 * hk + hv
            for g in range(tn // RET_DV):
                hd = (col - 2 * hk - is_g * hv) // RET_DV + g
                blk = p[:, g * RET_DV:(g + 1) * RET_DV]
                vg_ref[is_g * heads + hd, rows(r), :] = (jax.nn.silu(blk) if is_g else blk).astype(BF16)
        else:
            is_k = col >= hk
            if is_k:
                cos, sin = cos * k_scale, sin * k_scale
            for g in range(tn // RET_DK):
                hd = (col - is_k * hk) // RET_DK + g
                t1 = p[:, g * RET_DK:g * RET_DK + half]
                t2 = p[:, g * RET_DK + half:(g + 1) * RET_DK]
                r1 = t1 * cos - t2 * sin
                r2 = t1 * sin + t2 * cos
                qk_ref[is_k * heads + hd, rows(r), :half] = r1.astype(BF16)
                qk_ref[is_k * heads + hd, rows(r), half:] = r2.astype(BF16)
                if is_k:
                    for slab, dec in ((2 * heads + hd, sdec_f), (3 * heads + hd, sdec_b)):
                        d = dec[:, hd:hd + 1]
                        qk_ref[slab, rows(r), :half] = (r1 * d).astype(BF16)
                        qk_ref[slab, rows(r), half:] = (r2 * d).astype(BF16)

    order = ([c for c in range(0, 2 * hk, tn)] + [c for c in range(2 * hk + hv, 2 * hk + 2 * hv, tn)]
             + [c for c in range(2 * hk, 2 * hk + hv, tn)])
    cur = prologue(0)
    for r in range(splits):
        h, cos, sin = cur
        for idx, col in enumerate(order):
            p = _dot(h, w_ref[:, col:col + tn])
            if idx == 0 and r + 1 < splits:
                cur = prologue(r + 1)
            emit(r, col, p, cos, sin)


def _ret_proj(x2, nw, mod, layer, w_in, pos, inv_r, logit_f, logit_b, seq, heads):
    t, d = x2.shape
    n = w_in.shape[2]
    hk, hv = heads * RET_DK, heads * RET_DV
    tm, splits, tn = RET_PROJ_TM, RET_PROJ_GROUPS, RET_PROJ_TN
    assert hk % tn == 0 and hv % tn == 0 and tm // splits == RET_CHUNK and seq % tm == 0
    per_b = seq // tm
    vec = lambda k: pl.BlockSpec((None, None, 1, d), lambda i: (layer, i // per_b, 0, k))
    return pl.pallas_call(
        functools.partial(_ret_proj_kernel, heads=heads, k_scale=RET_DK ** -0.5, splits=splits, tn=tn),
        grid=(t // tm,),
        in_specs=[
            pl.BlockSpec((tm, d), lambda i: (i, 0)),
            pl.BlockSpec((None, None, 1, d), lambda i: (layer, 0, 0, 0)),
            vec(1), vec(0),
            pl.BlockSpec(memory_space=pl.ANY),
            pl.BlockSpec((tm, 1), lambda i: (i, 0)),
            _resident((1, LANES)),
            _resident((1, heads)), _resident((1, heads)),
        ],
        out_specs=[pl.BlockSpec((4 * heads, tm, RET_DK), lambda i: (0, i, 0)),
                   pl.BlockSpec((2 * heads, tm, RET_DV), lambda i: (0, i, 0))],
        out_shape=[jax.ShapeDtypeStruct((4 * heads, t, RET_DK), BF16),
                   jax.ShapeDtypeStruct((2 * heads, t, RET_DV), BF16)],
        scratch_shapes=[pltpu.VMEM((d, n), BF16), pltpu.VMEM(WEIGHT_STAGE, F32)],
        compiler_params=_cparams(("arbitrary",), RET_PROJ_VMEM_MIB),
        name="ret_proj",
    )(x2, nw, mod, mod, w_in, pos, inv_r, logit_f.reshape(1, heads), logit_b.reshape(1, heads))


def _retention_kernel(lf_ref, lb_ref, q_ref, k_ref, kf_ref, kb_ref, v_ref, sg_ref, gn_ref, o_ref,
                      sf_all, sb_all):
    c = RET_CHUNK
    n = q_ref.shape[0] // c
    lg_f = jax.nn.log_sigmoid(lf_ref[...].astype(F32))
    lg_b = jax.nn.log_sigmoid(lb_ref[...].astype(F32))
    row = lax.broadcasted_iota(jnp.int32, (c, c), 0).astype(F32)
    col = lax.broadcasted_iota(jnp.int32, (c, c), 1).astype(F32)
    pos = lax.broadcasted_iota(jnp.int32, (c, 1), 0).astype(F32)
    diff = row - col
    mask_f = diff >= 0
    mask_b = diff < 0
    decay = jnp.where(mask_f, jnp.exp(jnp.where(mask_f, diff * lg_f, 0.0)),
                      jnp.exp(jnp.where(mask_b, -diff * lg_b, 0.0)))
    cross_f = jnp.exp((pos + 1.0) * lg_f)
    cross_b = jnp.exp((c - pos) * lg_b)
    cdec_f = jnp.exp(c * lg_f)
    cdec_b = jnp.exp(c * lg_b)
    gn_w = gn_ref[...]

    def rows(ci):
        return slice(ci * c, (ci + 1) * c)

    sf = jnp.zeros(sf_all.shape[1:], F32)
    sb = jnp.zeros(sb_all.shape[1:], F32)
    for i in range(n):
        cb = n - 1 - i
        sf_all[i] = sf.astype(BF16)
        sb_all[cb] = sb.astype(BF16)
        if i + 1 < n:
            sf = sf * cdec_f + _dot_tn(kf_ref[rows(i), :], v_ref[rows(i), :])
            sb = sb * cdec_b + _dot_tn(kb_ref[rows(cb), :], v_ref[rows(cb), :])

    for ci in range(n):
        r = rows(ci)
        qi, vi = q_ref[r, :], v_ref[r, :]
        s = _dot_nt(qi, k_ref[r, :]) * decay
        y = (_dot(s.astype(BF16), vi) + _dot(qi, sf_all[ci]) * cross_f
             + _dot(qi, sb_all[ci]) * cross_b)
        mu = jnp.mean(y, axis=-1, keepdims=True)
        yc = y - mu
        var = jnp.mean(yc * yc, axis=-1, keepdims=True)
        yn = yc * lax.rsqrt(var + GN_EPS) * gn_w
        o_ref[r, :] = (sg_ref[r, :].astype(F32) * yn).astype(BF16)


def _retention(qk, vg, logit_f, logit_b, gn_w, batch, seq, heads):
    t = qk.shape[1]
    hv = heads * RET_DV
    scalar = pl.BlockSpec((None, 1, 1), lambda b, h: (h, 0, 0))
    slab = lambda m, w: pl.BlockSpec((None, seq, w), lambda b, h: (m * heads + h, b, 0))
    return pl.pallas_call(
        _retention_kernel,
        grid=(batch, heads),
        in_specs=[
            scalar, scalar,
            *[slab(m, RET_DK) for m in range(4)],
            slab(0, RET_DV), slab(1, RET_DV),
            pl.BlockSpec((1, RET_DV), lambda b, h: (0, h)),
        ],
        out_specs=pl.BlockSpec((seq, RET_DV), lambda b, h: (b, h)),
        out_shape=jax.ShapeDtypeStruct((t, hv), BF16),
        scratch_shapes=[
            pltpu.VMEM((seq // RET_CHUNK, RET_DK, RET_DV), BF16),
            pltpu.VMEM((seq // RET_CHUNK, RET_DK, RET_DV), BF16),
        ],
        compiler_params=_cparams(("parallel", "parallel"), RETENTION_VMEM_MIB),
        name="retention",
    )(logit_f.reshape(heads, 1, 1), logit_b.reshape(heads, 1, 1), *([qk] * 4), vg, vg,
      gn_w.reshape(1, hv))


def _stage_weight(w_hbm, w_scr, stage):
    rows, cols = w_scr.shape
    sr, sc = stage.shape
    for r0 in range(0, rows, sr):
        for c0 in range(0, cols, sc):
            pltpu.sync_copy(w_hbm.at[pl.ds(r0, sr), pl.ds(c0, sc)], stage)
            w_scr[r0:r0 + sr, c0:c0 + sc] = stage[...].astype(BF16)


def _mixer_mlp_kernel(*refs, emit_next, splits, tf, layer):
    if emit_next:
        (a_ref, wo_hbm, x_ref, ga_ref, nw1_ref, nw2_ref, scm_ref, shm_ref, w1_hbm, w2_hbm,
         gm_ref, nw3_ref, nwn_ref, scn_ref, shn_ref, xo_ref, hn_ref,
         wo_ref, w1_ref, w2_ref, stage) = refs
    else:
        (a_ref, wo_hbm, x_ref, ga_ref, nw1_ref, nw2_ref, scm_ref, shm_ref, w1_hbm, w2_hbm,
         gm_ref, nw3_ref, xo_ref, wo_ref, w1_ref, w2_ref, stage) = refs

    @pl.when(pl.program_id(0) == 0)
    def _():
        _stage_weight(wo_hbm.at[0], wo_ref, stage)
        _stage_weight(w1_hbm.at[layer], w1_ref, stage)
        _stage_weight(w2_hbm.at[layer], w2_ref, stage)

    rows_per = x_ref.shape[0] // splits
    n_chunks = w1_ref.shape[1] // tf

    def rows(r):
        return slice(r * rows_per, (r + 1) * rows_per)

    post_mix = ga_ref[...] * nw1_ref[...]
    pre_mlp = nw2_ref[...] * (1.0 + scm_ref[...])
    post_mlp = gm_ref[...] * nw3_ref[...]
    if emit_next:
        pre_next = nwn_ref[...] * (1.0 + scn_ref[...])

    def prologue(r):
        y = _dot(a_ref[rows(r), :], wo_ref[...])
        x1 = x_ref[rows(r), :] + _rms(y, post_mix)
        h = (_rms(x1, pre_mlp) + shm_ref[...]).astype(BF16)
        return x1, h

    def epilogue(r, x1, acc):
        x2 = x1 + _rms(acc, post_mlp)
        xo_ref[rows(r), :] = x2
        if emit_next:
            hn_ref[rows(r), :] = (_rms(x2, pre_next) + shn_ref[...]).astype(BF16)

    cur = prologue(0)
    done = None
    for r in range(splits):
        x1, h = cur
        acc = None
        for j in range(n_chunks):
            cols = slice(j * tf, (j + 1) * tf)
            u = jnp.square(jnp.maximum(_dot(h, w1_ref[:, cols]), 0.0)).astype(BF16)
            part = _dot(u, w2_ref[cols, :])
            acc = part if acc is None else acc + part
            if j == 0 and r + 1 < splits:
                cur = prologue(r + 1)
            if j == min(1, n_chunks - 1) and done is not None:
                epilogue(*done)
                done = None
        done = (r, x1, acc)
    epilogue(*done)


def _resident(shape):
    return pl.BlockSpec(shape, lambda *_: (0,) * len(shape), pipeline_mode=pl.Buffered(1))


def _mixer_mlp(a, w_out, x2, w1, w2, nw, mod, layer, seq, emit_next):
    t, d = x2.shape
    kd = a.shape[1]
    dff = w1.shape[2]
    tf = MLP_TF
    hbm = pl.BlockSpec(memory_space=pl.ANY)
    if emit_next:
        tm, splits, vmem_mib = MLP_NEXT_TM, MLP_NEXT_GROUPS, MLP_NEXT_VMEM_MIB
    else:
        tm, splits, vmem_mib = MLP_LAST_TM, MLP_LAST_GROUPS, MLP_LAST_VMEM_MIB
    per_b = seq // tm
    vec = lambda l, k: pl.BlockSpec((None, None, 1, d), lambda i: (l, i // per_b, 0, k))
    nvec = lambda l, k: pl.BlockSpec((None, None, 1, d), lambda i: (l, k, 0, 0))
    row = pl.BlockSpec((tm, d), lambda i: (i, 0))
    in_specs = [
        pl.BlockSpec((tm, kd), lambda i: (i, 0)), hbm, row,
        vec(layer, 2), nvec(layer, 1), nvec(layer, 2), vec(layer, 4), vec(layer, 3),
        hbm, hbm, vec(layer, 5), nvec(layer, 3),
    ]
    args = [a, w_out, x2, mod, nw, nw, mod, mod, w1, w2, mod, nw]
    out_specs = [row]
    out_shape = [jax.ShapeDtypeStruct((t, d), F32)]
    if emit_next:
        in_specs += [nvec(layer + 1, 0), vec(layer + 1, 1), vec(layer + 1, 0)]
        args += [nw, mod, mod]
        out_specs.append(row)
        out_shape.append(jax.ShapeDtypeStruct((t, d), BF16))
    return pl.pallas_call(
        functools.partial(_mixer_mlp_kernel, emit_next=emit_next, splits=splits, tf=tf,
                          layer=layer),
        grid=(t // tm,),
        in_specs=in_specs,
        out_specs=out_specs,
        out_shape=out_shape,
        scratch_shapes=[pltpu.VMEM((kd, d), BF16), pltpu.VMEM((d, dff), BF16),
                        pltpu.VMEM((dff, d), BF16), pltpu.VMEM(WEIGHT_STAGE, F32)],
        compiler_params=_cparams(("arbitrary",), vmem_mib),
        name="mixer_mlp_next" if emit_next else "mixer_mlp_last",
    )(*args)


def _mla_proj_kernel(h_ref, wc_ref, qn_ref, kvn_ref, wuq_ref, wukv_ref, pos_ref, inv_ref,
                     sign_ref, q_ref, k_ref, v_ref, *, heads, splits):
    q_lora = qn_ref.shape[1]
    kv_lora = kvn_ref.shape[1]
    nope_w = heads * MLA_NOPE
    rows_per = h_ref.shape[0] // splits
    lane = lax.broadcasted_iota(jnp.int32, (1, LANES), 1)
    first_half = lane < MLA_ROPE // 2
    pair = rows_per // 2

    def compress(r):
        rows = slice(r * rows_per, (r + 1) * rows_per)
        c_all = _dot(h_ref[rows, :], wc_ref[...])
        cq = _rms(c_all[:, :q_lora], qn_ref[...]).astype(BF16)
        ckv = _rms(c_all[:, q_lora:q_lora + kv_lora], kvn_ref[...]).astype(BF16)
        pos_a = pos_ref[r * rows_per:r * rows_per + pair, :]
        pos_b = pos_ref[r * rows_per + pair:(r + 1) * rows_per, :]
        ang = jnp.where(lane < MLA_ROPE, pos_a, pos_b).astype(F32) * inv_ref[...]
        cos2 = jnp.cos(ang)
        sin2 = jnp.sin(ang)
        cos = jnp.concatenate([cos2, pltpu.roll(cos2, MLA_ROPE, 1)], axis=0)
        sin = jnp.concatenate([sin2, pltpu.roll(sin2, MLA_ROPE, 1)], axis=0) * sign_ref[...]
        return cq, ckv, c_all[:, q_lora + kv_lora:], cos, sin

    cur = compress(0)
    for r in range(splits):
        rows = slice(r * rows_per, (r + 1) * rows_per)
        cq, ckv, kr_raw, cos, sin = cur

        def rotary(t, cos=cos, sin=sin):
            partner = jnp.where(first_half, pltpu.roll(t, LANES - MLA_ROPE // 2, 1),
                                pltpu.roll(t, MLA_ROPE // 2, 1))
            return t * cos + partner * sin

        kr = rotary(kr_raw).astype(BF16)
        for hd in range(heads):
            lo = hd * MLA_HEAD_PAD
            if hd % 2 == 0:
                kn = _dot(ckv, wukv_ref[:, hd * LANES:(hd + 2) * LANES])
                vv = _dot(ckv, wukv_ref[:, nope_w + hd * MLA_V:nope_w + (hd + 2) * MLA_V])
            if hd == 2 and r + 1 < splits:
                cur = compress(r + 1)
            qh = _dot(cq, wuq_ref[:, lo:lo + MLA_HEAD_PAD])
            q_ref[hd, rows, :LANES] = qh[:, :LANES].astype(BF16)
            q_ref[hd, rows, LANES:] = rotary(qh[:, LANES:]).astype(BF16)
            k_ref[hd, rows, :LANES] = kn[:, (hd % 2) * LANES:(hd % 2 + 1) * LANES].astype(BF16)
            k_ref[hd, rows, LANES:] = kr
            v_ref[hd, rows, :] = vv[:, (hd % 2) * MLA_V:(hd % 2 + 1) * MLA_V].astype(BF16)


def _mla_proj(h, wc, q_norm, kv_norm, wuq, wukv, pos, inv_m, sign_m, heads):
    t, d = h.shape
    tm, splits = MLA_PROJ_TM, MLA_PROJ_GROUPS
    qn = q_norm.reshape(1, -1) * float(MLA_DQK ** -0.5 * np.log2(np.e))
    kvn = kv_norm.reshape(1, -1)
    row = lambda w: pl.BlockSpec((tm, w), lambda i: (i, 0))
    return pl.pallas_call(
        functools.partial(_mla_proj_kernel, heads=heads, splits=splits),
        grid=(t // tm,),
        in_specs=[row(d), _resident(wc.shape), _resident(qn.shape), _resident(kvn.shape),
                  _resident(wuq.shape), _resident(wukv.shape), row(1), _resident((1, LANES)),
                  _resident((1, LANES))],
        out_specs=[pl.BlockSpec((heads, tm, w), lambda i: (0, i, 0))
                   for w in (MLA_HEAD_PAD, MLA_HEAD_PAD, MLA_V)],
        out_shape=[jax.ShapeDtypeStruct((heads, t, w), BF16)
                   for w in (MLA_HEAD_PAD, MLA_HEAD_PAD, MLA_V)],
        compiler_params=_cparams(("parallel",), MLA_PROJ_VMEM_MIB),
        name="mla_proj",
    )(h, wc, qn, kvn, wuq, wukv, pos, inv_m, sign_m)


def _attn_kernel(q_ref, k_ref, v_ref, o_ref, v1_scr, *, sub, heads_per_step):
    seq = q_ref.shape[1]
    n_sub = seq // sub
    for hh in range(heads_per_step):
        v1_scr[hh, :, :MLA_V] = v_ref[hh]
        v1_scr[hh, :, MLA_V:] = jnp.ones((seq, MLA_V), BF16)

    def scores(i):
        hh, r = divmod(i, n_sub)
        return _dot_nt(q_ref[hh, r * sub:(r + 1) * sub, :], k_ref[hh])

    s_next = scores(0)
    for i in range(heads_per_step * n_sub):
        hh, r = divmod(i, n_sub)
        s = s_next
        if i + 1 < heads_per_step * n_sub:
            s_next = scores(i + 1)
        p = jnp.exp2(s - jnp.max(s, axis=-1, keepdims=True))
        ov = _dot(p.astype(BF16), v1_scr[hh])
        o_ref[r * sub:(r + 1) * sub, hh * MLA_V:(hh + 1) * MLA_V] = (
            ov[:, :MLA_V] / ov[:, MLA_V:MLA_V + 1]).astype(BF16)


def _attention(q, k, v, batch, seq, heads):
    t = q.shape[1]
    sub, hp = ATTN_SUB, ATTN_HEADS_PER_STEP
    slab = lambda w: pl.BlockSpec((hp, seq, w), lambda b, h: (h, b, 0))
    return pl.pallas_call(
        functools.partial(_attn_kernel, sub=sub, heads_per_step=hp),
        grid=(batch, heads // hp),
        in_specs=[slab(MLA_HEAD_PAD), slab(MLA_HEAD_PAD), slab(MLA_V)],
        out_specs=pl.BlockSpec((seq, hp * MLA_V), lambda b, h: (b, h)),
        out_shape=jax.ShapeDtypeStruct((t, heads * MLA_V), BF16),
        scratch_shapes=[pltpu.VMEM((hp, seq, 2 * MLA_V), BF16)],
        compiler_params=_cparams(("parallel", "parallel"), ATTN_VMEM_MIB),
        name="mla_attention",
    )(q, k, v)


def _mla_weights(w_in, w_uq, w_ukv, heads):
    q_lora = w_uq.shape[0]
    kv_lora = w_ukv.shape[0]
    wc = jnp.pad(w_in, [(0, 0), (0, LANES - MLA_ROPE)])
    uq = w_uq.reshape(q_lora, heads, MLA_DQK)
    wuq = jnp.pad(uq, [(0, 0), (0, 0), (0, MLA_HEAD_PAD - MLA_DQK)]).reshape(q_lora, heads * MLA_HEAD_PAD)
    ukv = w_ukv.reshape(kv_lora, heads, MLA_NOPE + MLA_V)
    wukv = jnp.concatenate([
        ukv[..., :MLA_NOPE].reshape(kv_lora, heads * MLA_NOPE),
        ukv[..., MLA_NOPE:].reshape(kv_lora, heads * MLA_V),
    ], axis=-1)
    return [w.astype(BF16) for w in (wc, wuq, wukv)]


def kernel(x, c, positions, norm_w, ada_w, ada_b, ret_w_in, ret_decay_logit_fwd,
           ret_decay_logit_bwd, ret_gn_w, ret_w_out, mla_w_in, mla_q_norm, mla_w_uq,
           mla_kv_norm, mla_w_ukv, mla_w_out, mlp_w1, mlp_w2):
    batch, seq, d = x.shape
    depth = norm_w.shape[0]
    assert depth == 2
    t = batch * seq
    ret_heads = ret_decay_logit_fwd.shape[1]
    mla_heads = mla_w_out.shape[1] // MLA_V

    mod = _ada_mod(c, ada_w, ada_b).reshape(depth, batch, 1, N_ADA * d)
    nw = norm_w.reshape(depth, 4, 1, d)
    inv_r, inv_m, sign_m = _rope_freqs()
    pos = positions.reshape(t, 1)
    x2 = x.reshape(t, d)

    qk, vg = _ret_proj(x2, nw, mod, 0, ret_w_in, pos, inv_r,
                       ret_decay_logit_fwd[0], ret_decay_logit_bwd[0], seq, ret_heads)
    y = _retention(qk, vg, ret_decay_logit_fwd[0], ret_decay_logit_bwd[0], ret_gn_w[0],
                   batch, seq, ret_heads)
    x2, h = _mixer_mlp(y, ret_w_out, x2, mlp_w1, mlp_w2, nw, mod, 0, seq, True)

    wc, wuq, wukv = _mla_weights(mla_w_in[0], mla_w_uq[0], mla_w_ukv[0], mla_heads)
    q, k, v = _mla_proj(h, wc, mla_q_norm[0], mla_kv_norm[0], wuq, wukv, pos, inv_m, sign_m,
                        mla_heads)
    o = _attention(q, k, v, batch, seq, mla_heads)
    (x2,) = _mixer_mlp(o, mla_w_out, x2, mlp_w1, mlp_w2, nw, mod, 1, seq, False)
    return x2.reshape(batch, seq, d)
```

```python
import functools

import numpy as np
import jax
import jax.numpy as jnp
from jax import lax
from jax.experimental import pallas as pl
from jax.experimental.pallas import tpu as pltpu

F32 = jnp.float32
BF16 = jnp.bfloat16

RMS_EPS = 1e-6
GN_EPS = 1e-5
ROPE_THETA = 10000.0
N_ADA = 6

RET_DK = 256
RET_DV = 512
RET_CHUNK = 256

MLA_NOPE = 128
MLA_ROPE = 64
MLA_V = 128
MLA_DQK = MLA_NOPE + MLA_ROPE
MLA_HEAD_PAD = 256

LANES = 128
MIB = 1024 * 1024

ADA_TN, ADA_VMEM_MIB = 1024, 32
RET_PROJ_TM, RET_PROJ_GROUPS, RET_PROJ_TN, RET_PROJ_VMEM_MIB = 512, 2, 1024, 56
RETENTION_VMEM_MIB = 48
MLP_TF = 1024
MLP_NEXT_TM, MLP_NEXT_GROUPS, MLP_NEXT_VMEM_MIB = 512, 2, 56
MLP_LAST_TM, MLP_LAST_GROUPS, MLP_LAST_VMEM_MIB = 1024, 4, 58
MLA_PROJ_TM, MLA_PROJ_GROUPS, MLA_PROJ_VMEM_MIB = 1024, 2, 48
ATTN_SUB, ATTN_HEADS_PER_STEP, ATTN_VMEM_MIB = 256, 2, 48
WEIGHT_STAGE = (2, 512, 1024)


def _cparams(semantics, vmem_mib):
    return pltpu.CompilerParams(dimension_semantics=semantics,
                                vmem_limit_bytes=vmem_mib * MIB)


def _rms(t, w):
    return t * lax.rsqrt(jnp.mean(t * t, axis=-1, keepdims=True) + RMS_EPS) * w


def _dot(a, b):
    return jnp.dot(a, b, preferred_element_type=F32)


def _dot_nt(a, b):
    return lax.dot_general(a, b, (((1,), (1,)), ((), ())), preferred_element_type=F32)


def _dot_tn(a, b):
    return lax.dot_general(a, b, (((0,), (0,)), ((), ())), preferred_element_type=F32)


def _ada_kernel(c_ref, w_ref, b_ref, o_ref):
    c_act = jax.nn.silu(c_ref[...])
    o_ref[...] = _dot(c_act.astype(BF16), w_ref[...].astype(BF16)) + b_ref[...]


def _ada_mod(c, ada_w, ada_b):
    depth, d, n = ada_w.shape
    b = c.shape[0]
    tn = ADA_TN
    return pl.pallas_call(
        _ada_kernel,
        grid=(depth, n // tn),
        in_specs=[
            pl.BlockSpec((b, d), lambda l, j: (0, 0)),
            pl.BlockSpec((None, d, tn), lambda l, j: (l, 0, j)),
            pl.BlockSpec((None, 1, tn), lambda l, j: (l, 0, j)),
        ],
        out_specs=pl.BlockSpec((None, b, tn), lambda l, j: (l, 0, j)),
        out_shape=jax.ShapeDtypeStruct((depth, b, n), F32),
        compiler_params=_cparams(("arbitrary", "arbitrary"), ADA_VMEM_MIB),
        name="ada_mod",
    )(c, ada_w, ada_b.reshape(depth, 1, n))


def _rope_freqs():
    half_m = MLA_ROPE // 2
    assert RET_DK // 2 == LANES
    inv_r = ROPE_THETA ** (-jnp.arange(0, RET_DK, 2, dtype=F32) / RET_DK)
    inv_m = ROPE_THETA ** (-jnp.arange(0, MLA_ROPE, 2, dtype=F32) / MLA_ROPE)
    inv_m = jnp.tile(inv_m, LANES // half_m)
    sign_m = jnp.concatenate([-jnp.ones((half_m,), F32), jnp.ones((half_m,), F32),
                              jnp.zeros((LANES - MLA_ROPE,), F32)])
    return inv_r.reshape(1, LANES), inv_m.reshape(1, LANES), sign_m.reshape(1, LANES)


def _ret_proj_kernel(x_ref, nw_ref, sc_ref, sh_ref, w_hbm, pos_ref, inv_ref, lf_ref, lb_ref,
                     qk_ref, vg_ref, w_ref, stage, sem, *, heads, k_scale, splits, tn):
    @pl.when(pl.program_id(0) == 0)
    def _():
        _stage_weight(w_hbm.at[0], w_ref, stage, sem)

    rows_per = x_ref.shape[0] // splits
    hk, hv = heads * RET_DK, heads * RET_DV
    half = RET_DK // 2
    local = lax.broadcasted_iota(jnp.int32, (rows_per, 1), 0).astype(F32)
    sdec_f = jnp.exp((rows_per - 1.0 - local) * jax.nn.log_sigmoid(lf_ref[...]))
    sdec_b = jnp.exp(local * jax.nn.log_sigmoid(lb_ref[...]))

    def rows(r):
        return slice(r * rows_per, (r + 1) * rows_per)

    pre_gain = nw_ref[...] * (1.0 + sc_ref[...])

    def prologue(r):
        h = (_rms(x_ref[rows(r), :], pre_gain) + sh_ref[...]).astype(BF16)
        ang = pos_ref[rows(r), :].astype(F32) * inv_ref[...]
        return h, jnp.cos(ang), jnp.sin(ang)

    def emit(r, col, p, cos, sin):
        if col >= 2 * hk:
            is_g = col >= 2 * hk + hv
            for g in range(tn // RET_DV):
                hd = (col - 2 * hk - is_g * hv) // RET_DV + g
                blk = p[:, g * RET_DV:(g + 1) * RET_DV]
                vg_ref[is_g * heads + hd, rows(r), :] = (jax.nn.silu(blk) if is_g else blk).astype(BF16)
        else:
            is_k = col >= hk
            if is_k:
                cos, sin = cos * k_scale, sin * k_scale
            for g in range(tn // RET_DK):
                hd = (col - is_k * hk) // RET_DK + g
                t1 = p[:, g * RET_DK:g * RET_DK + half]
                t2 = p[:, g * RET_DK + half:(g + 1) * RET_DK]
                r1 = t1 * cos - t2 * sin
                r2 = t1 * sin + t2 * cos
                qk_ref[is_k * heads + hd, rows(r), :half] = r1.astype(BF16)
                qk_ref[is_k * heads + hd, rows(r), half:] = r2.astype(BF16)
                if is_k:
                    for slab, dec in ((2 * heads + hd, sdec_f), (3 * heads + hd, sdec_b)):
                        d = dec[:, hd:hd + 1]
                        qk_ref[slab, rows(r), :half] = (r1 * d).astype(BF16)
                        qk_ref[slab, rows(r), half:] = (r2 * d).astype(BF16)

    order = ([c for c in range(0, 2 * hk, tn)] + [c for c in range(2 * hk + hv, 2 * hk + 2 * hv, tn)]
             + [c for c in range(2 * hk, 2 * hk + hv, tn)])
    cur = prologue(0)
    for r in range(splits):
        h, cos, sin = cur
        for idx, col in enumerate(order):
            p = _dot(h, w_ref[:, col:col + tn])
            if idx == 0 and r + 1 < splits:
                cur = prologue(r + 1)
            emit(r, col, p, cos, sin)


def _ret_proj(x2, nw, mod, layer, w_in, pos, inv_r, logit_f, logit_b, seq, heads):
    t, d = x2.shape
    n = w_in.shape[2]
    hk, hv = heads * RET_DK, heads * RET_DV
    tm, splits, tn = RET_PROJ_TM, RET_PROJ_GROUPS, RET_PROJ_TN
    assert hk % tn == 0 and hv % tn == 0 and tm // splits == RET_CHUNK and seq % tm == 0
    per_b = seq // tm
    vec = lambda k: pl.BlockSpec((None, None, 1, d), lambda i: (layer, i // per_b, 0, k))
    return pl.pallas_call(
        functools.partial(_ret_proj_kernel, heads=heads, k_scale=RET_DK ** -0.5, splits=splits, tn=tn),
        grid=(t // tm,),
        in_specs=[
            pl.BlockSpec((tm, d), lambda i: (i, 0)),
            pl.BlockSpec((None, None, 1, d), lambda i: (layer, 0, 0, 0)),
            vec(1), vec(0),
            pl.BlockSpec(memory_space=pl.ANY),
            pl.BlockSpec((tm, 1), lambda i: (i, 0)),
            _resident((1, LANES)),
            _resident((1, heads)), _resident((1, heads)),
        ],
        out_specs=[pl.BlockSpec((4 * heads, tm, RET_DK), lambda i: (0, i, 0)),
                   pl.BlockSpec((2 * heads, tm, RET_DV), lambda i: (0, i, 0))],
        out_shape=[jax.ShapeDtypeStruct((4 * heads, t, RET_DK), BF16),
                   jax.ShapeDtypeStruct((2 * heads, t, RET_DV), BF16)],
        scratch_shapes=[pltpu.VMEM((d, n), BF16), pltpu.VMEM(WEIGHT_STAGE, F32),
                        pltpu.SemaphoreType.DMA((2,))],
        compiler_params=_cparams(("arbitrary",), RET_PROJ_VMEM_MIB),
        name="ret_proj",
    )(x2, nw, mod, mod, w_in, pos, inv_r, logit_f.reshape(1, heads), logit_b.reshape(1, heads))


def _retention_kernel(lf_ref, lb_ref, q_ref, k_ref, kf_ref, kb_ref, v_ref, sg_ref, gn_ref, o_ref,
                      sf_all, sb_all):
    c = RET_CHUNK
    n = q_ref.shape[0] // c
    lg_f = jax.nn.log_sigmoid(lf_ref[...].astype(F32))
    lg_b = jax.nn.log_sigmoid(lb_ref[...].astype(F32))
    row = lax.broadcasted_iota(jnp.int32, (c, c), 0).astype(F32)
    col = lax.broadcasted_iota(jnp.int32, (c, c), 1).astype(F32)
    pos = lax.broadcasted_iota(jnp.int32, (c, 1), 0).astype(F32)
    diff = row - col
    mask_f = diff >= 0
    mask_b = diff < 0
    decay = jnp.where(mask_f, jnp.exp(jnp.where(mask_f, diff * lg_f, 0.0)),
                      jnp.exp(jnp.where(mask_b, -diff * lg_b, 0.0)))
    cross_f = jnp.exp((pos + 1.0) * lg_f)
    cross_b = jnp.exp((c - pos) * lg_b)
    cdec_f = jnp.exp(c * lg_f)
    cdec_b = jnp.exp(c * lg_b)
    gn_w = gn_ref[...]

    def rows(ci):
        return slice(ci * c, (ci + 1) * c)

    sf = jnp.zeros(sf_all.shape[1:], F32)
    sb = jnp.zeros(sb_all.shape[1:], F32)
    for i in range(n):
        cb = n - 1 - i
        sf_all[i] = sf.astype(BF16)
        sb_all[cb] = sb.astype(BF16)
        if i + 1 < n:
            sf = sf * cdec_f + _dot_tn(kf_ref[rows(i), :], v_ref[rows(i), :])
            sb = sb * cdec_b + _dot_tn(kb_ref[rows(cb), :], v_ref[rows(cb), :])

    for ci in range(n):
        r = rows(ci)
        qi, vi = q_ref[r, :], v_ref[r, :]
        s = _dot_nt(qi, k_ref[r, :]) * decay
        y = (_dot(s.astype(BF16), vi) + _dot(qi, sf_all[ci]) * cross_f
             + _dot(qi, sb_all[ci]) * cross_b)
        mu = jnp.mean(y, axis=-1, keepdims=True)
        yc = y - mu
        var = jnp.mean(yc * yc, axis=-1, keepdims=True)
        yn = yc * lax.rsqrt(var + GN_EPS) * gn_w
        o_ref[r, :] = (sg_ref[r, :].astype(F32) * yn).astype(BF16)


def _retention(qk, vg, logit_f, logit_b, gn_w, batch, seq, heads):
    t = qk.shape[1]
    hv = heads * RET_DV
    scalar = pl.BlockSpec((None, 1, 1), lambda b, h: (h, 0, 0))
    slab = lambda m, w: pl.BlockSpec((None, seq, w), lambda b, h: (m * heads + h, b, 0))
    return pl.pallas_call(
        _retention_kernel,
        grid=(batch, heads),
        in_specs=[
            scalar, scalar,
            *[slab(m, RET_DK) for m in range(4)],
            slab(0, RET_DV), slab(1, RET_DV),
            pl.BlockSpec((1, RET_DV), lambda b, h: (0, h)),
        ],
        out_specs=pl.BlockSpec((seq, RET_DV), lambda b, h: (b, h)),
        out_shape=jax.ShapeDtypeStruct((t, hv), BF16),
        scratch_shapes=[
            pltpu.VMEM((seq // RET_CHUNK, RET_DK, RET_DV), BF16),
            pltpu.VMEM((seq // RET_CHUNK, RET_DK, RET_DV), BF16),
        ],
        compiler_params=_cparams(("parallel", "parallel"), RETENTION_VMEM_MIB),
        name="retention",
    )(logit_f.reshape(heads, 1, 1), logit_b.reshape(heads, 1, 1), *([qk] * 4), vg, vg,
      gn_w.reshape(1, hv))


def _stage_weight(w_hbm, w_scr, stage, sem):
    rows, cols = w_scr.shape
    _, sr, sc = stage.shape
    tiles = [(r0, c0) for r0 in range(0, rows, sr) for c0 in range(0, cols, sc)]

    def copy(i):
        r0, c0 = tiles[i]
        return pltpu.make_async_copy(w_hbm.at[pl.ds(r0, sr), pl.ds(c0, sc)],
                                     stage.at[i % 2], sem.at[i % 2])

    copy(0).start()
    for i, (r0, c0) in enumerate(tiles):
        if i + 1 < len(tiles):
            copy(i + 1).start()
        copy(i).wait()
        w_scr[r0:r0 + sr, c0:c0 + sc] = stage[i % 2].astype(BF16)


def _mixer_mlp_kernel(*refs, emit_next, splits, tf, layer):
    if emit_next:
        (a_ref, wo_hbm, x_ref, ga_ref, nw1_ref, nw2_ref, scm_ref, shm_ref, w1_hbm, w2_hbm,
         gm_ref, nw3_ref, nwn_ref, scn_ref, shn_ref, xo_ref, hn_ref,
         wo_ref, w1_ref, w2_ref, stage, sem) = refs
    else:
        (a_ref, wo_hbm, x_ref, ga_ref, nw1_ref, nw2_ref, scm_ref, shm_ref, w1_hbm, w2_hbm,
         gm_ref, nw3_ref, xo_ref, wo_ref, w1_ref, w2_ref, stage, sem) = refs

    @pl.when(pl.program_id(0) == 0)
    def _():
        _stage_weight(wo_hbm.at[0], wo_ref, stage, sem)
        _stage_weight(w1_hbm.at[layer], w1_ref, stage, sem)
        _stage_weight(w2_hbm.at[layer], w2_ref, stage, sem)

    rows_per = x_ref.shape[0] // splits
    n_chunks = w1_ref.shape[1] // tf

    def rows(r):
        return slice(r * rows_per, (r + 1) * rows_per)

    post_mix = ga_ref[...] * nw1_ref[...]
    pre_mlp = nw2_ref[...] * (1.0 + scm_ref[...])
    post_mlp = gm_ref[...] * nw3_ref[...]
    if emit_next:
        pre_next = nwn_ref[...] * (1.0 + scn_ref[...])

    def prologue(r):
        y = _dot(a_ref[rows(r), :], wo_ref[...])
        x1 = x_ref[rows(r), :] + _rms(y, post_mix)
        h = (_rms(x1, pre_mlp) + shm_ref[...]).astype(BF16)
        return x1, h

    def epilogue(r, x1, acc):
        x2 = x1 + _rms(acc, post_mlp)
        xo_ref[rows(r), :] = x2
        if emit_next:
            hn_ref[rows(r), :] = (_rms(x2, pre_next) + shn_ref[...]).astype(BF16)

    cur = prologue(0)
    done = None
    for r in range(splits):
        x1, h = cur
        acc = None
        for j in range(n_chunks):
            cols = slice(j * tf, (j + 1) * tf)
            u = jnp.square(jnp.maximum(_dot(h, w1_ref[:, cols]), 0.0)).astype(BF16)
            part = _dot(u, w2_ref[cols, :])
            acc = part if acc is None else acc + part
            if j == 0 and r + 1 < splits:
                cur = prologue(r + 1)
            if j == min(1, n_chunks - 1) and done is not None:
                epilogue(*done)
                done = None
        done = (r, x1, acc)
    epilogue(*done)


def _resident(shape):
    return pl.BlockSpec(shape, lambda *_: (0,) * len(shape), pipeline_mode=pl.Buffered(1))


def _mixer_mlp(a, w_out, x2, w1, w2, nw, mod, layer, seq, emit_next):
    t, d = x2.shape
    kd = a.shape[1]
    dff = w1.shape[2]
    tf = MLP_TF
    hbm = pl.BlockSpec(memory_space=pl.ANY)
    if emit_next:
        tm, splits, vmem_mib = MLP_NEXT_TM, MLP_NEXT_GROUPS, MLP_NEXT_VMEM_MIB
    else:
        tm, splits, vmem_mib = MLP_LAST_TM, MLP_LAST_GROUPS, MLP_LAST_VMEM_MIB
    per_b = seq // tm
    vec = lambda l, k: pl.BlockSpec((None, None, 1, d), lambda i: (l, i // per_b, 0, k))
    nvec = lambda l, k: pl.BlockSpec((None, None, 1, d), lambda i: (l, k, 0, 0))
    row = pl.BlockSpec((tm, d), lambda i: (i, 0))
    in_specs = [
        pl.BlockSpec((tm, kd), lambda i: (i, 0)), hbm, row,
        vec(layer, 2), nvec(layer, 1), nvec(layer, 2), vec(layer, 4), vec(layer, 3),
        hbm, hbm, vec(layer, 5), nvec(layer, 3),
    ]
    args = [a, w_out, x2, mod, nw, nw, mod, mod, w1, w2, mod, nw]
    out_specs = [row]
    out_shape = [jax.ShapeDtypeStruct((t, d), F32)]
    if emit_next:
        in_specs += [nvec(layer + 1, 0), vec(layer + 1, 1), vec(layer + 1, 0)]
        args += [nw, mod, mod]
        out_specs.append(row)
        out_shape.append(jax.ShapeDtypeStruct((t, d), BF16))
    return pl.pallas_call(
        functools.partial(_mixer_mlp_kernel, emit_next=emit_next, splits=splits, tf=tf,
                          layer=layer),
        grid=(t // tm,),
        in_specs=in_specs,
        out_specs=out_specs,
        out_shape=out_shape,
        scratch_shapes=[pltpu.VMEM((kd, d), BF16), pltpu.VMEM((d, dff), BF16),
                        pltpu.VMEM((dff, d), BF16), pltpu.VMEM(WEIGHT_STAGE, F32),
                        pltpu.SemaphoreType.DMA((2,))],
        compiler_params=_cparams(("arbitrary",), vmem_mib),
        name="mixer_mlp_next" if emit_next else "mixer_mlp_last",
    )(*args)


def _mla_proj_kernel(h_ref, wc_ref, qn_ref, kvn_ref, wuq_ref, wukv_ref, pos_ref, inv_ref,
                     sign_ref, q_ref, k_ref, v_ref, *, heads, splits):
    q_lora = qn_ref.shape[1]
    kv_lora = kvn_ref.shape[1]
    nope_w = heads * MLA_NOPE
    rows_per = h_ref.shape[0] // splits
    lane = lax.broadcasted_iota(jnp.int32, (1, LANES), 1)
    first_half = lane < MLA_ROPE // 2
    pair = rows_per // 2

    def compress(r):
        rows = slice(r * rows_per, (r + 1) * rows_per)
        c_all = _dot(h_ref[rows, :], wc_ref[...])
        cq = _rms(c_all[:, :q_lora], qn_ref[...]).astype(BF16)
        ckv = _rms(c_all[:, q_lora:q_lora + kv_lora], kvn_ref[...]).astype(BF16)
        pos_a = pos_ref[r * rows_per:r * rows_per + pair, :]
        pos_b = pos_ref[r * rows_per + pair:(r + 1) * rows_per, :]
        ang = jnp.where(lane < MLA_ROPE, pos_a, pos_b).astype(F32) * inv_ref[...]
        cos2 = jnp.cos(ang)
        sin2 = jnp.sin(ang)
        cos = jnp.concatenate([cos2, pltpu.roll(cos2, MLA_ROPE, 1)], axis=0)
        sin = jnp.concatenate([sin2, pltpu.roll(sin2, MLA_ROPE, 1)], axis=0) * sign_ref[...]
        return cq, ckv, c_all[:, q_lora + kv_lora:], cos, sin

    cur = compress(0)
    for r in range(splits):
        rows = slice(r * rows_per, (r + 1) * rows_per)
        cq, ckv, kr_raw, cos, sin = cur

        def rotary(t, cos=cos, sin=sin):
            partner = jnp.where(first_half, pltpu.roll(t, LANES - MLA_ROPE // 2, 1),
                                pltpu.roll(t, MLA_ROPE // 2, 1))
            return t * cos + partner * sin

        kr = rotary(kr_raw).astype(BF16)
        for hd in range(heads):
            lo = hd * MLA_HEAD_PAD
            if hd % 2 == 0:
                kn = _dot(ckv, wukv_ref[:, hd * LANES:(hd + 2) * LANES])
                vv = _dot(ckv, wukv_ref[:, nope_w + hd * MLA_V:nope_w + (hd + 2) * MLA_V])
            if hd == 2 and r + 1 < splits:
                cur = compress(r + 1)
            qh = _dot(cq, wuq_ref[:, lo:lo + MLA_HEAD_PAD])
            q_ref[hd, rows, :LANES] = qh[:, :LANES].astype(BF16)
            q_ref[hd, rows, LANES:] = rotary(qh[:, LANES:]).astype(BF16)
            k_ref[hd, rows, :LANES] = kn[:, (hd % 2) * LANES:(hd % 2 + 1) * LANES].astype(BF16)
            k_ref[hd, rows, LANES:] = kr
            v_ref[hd, rows, :] = vv[:, (hd % 2) * MLA_V:(hd % 2 + 1) * MLA_V].astype(BF16)


def _mla_proj(h, wc, q_norm, kv_norm, wuq, wukv, pos, inv_m, sign_m, heads):
    t, d = h.shape
    tm, splits = MLA_PROJ_TM, MLA_PROJ_GROUPS
    qn = q_norm.reshape(1, -1) * float(MLA_DQK ** -0.5 * np.log2(np.e))
    kvn = kv_norm.reshape(1, -1)
    row = lambda w: pl.BlockSpec((tm, w), lambda i: (i, 0))
    return pl.pallas_call(
        functools.partial(_mla_proj_kernel, heads=heads, splits=splits),
        grid=(t // tm,),
        in_specs=[row(d), _resident(wc.shape), _resident(qn.shape), _resident(kvn.shape),
                  _resident(wuq.shape), _resident(wukv.shape), row(1), _resident((1, LANES)),
                  _resident((1, LANES))],
        out_specs=[pl.BlockSpec((heads, tm, w), lambda i: (0, i, 0))
                   for w in (MLA_HEAD_PAD, MLA_HEAD_PAD, MLA_V)],
        out_shape=[jax.ShapeDtypeStruct((heads, t, w), BF16)
                   for w in (MLA_HEAD_PAD, MLA_HEAD_PAD, MLA_V)],
        compiler_params=_cparams(("parallel",), MLA_PROJ_VMEM_MIB),
        name="mla_proj",
    )(h, wc, qn, kvn, wuq, wukv, pos, inv_m, sign_m)


def _attn_kernel(q_ref, k_ref, v_ref, o_ref, v1_scr, *, sub, heads_per_step):
    seq = q_ref.shape[1]
    n_sub = seq // sub
    for hh in range(heads_per_step):
        v1_scr[hh, :, :MLA_V] = v_ref[hh]
        v1_scr[hh, :, MLA_V:] = jnp.ones((seq, MLA_V), BF16)

    def scores(i):
        hh, r = divmod(i, n_sub)
        return _dot_nt(q_ref[hh, r * sub:(r + 1) * sub, :], k_ref[hh])

    s_next = scores(0)
    for i in range(heads_per_step * n_sub):
        hh, r = divmod(i, n_sub)
        s = s_next
        if i + 1 < heads_per_step * n_sub:
            s_next = scores(i + 1)
        p = jnp.exp2(s - jnp.max(s, axis=-1, keepdims=True))
        ov = _dot(p.astype(BF16), v1_scr[hh])
        o_ref[r * sub:(r + 1) * sub, hh * MLA_V:(hh + 1) * MLA_V] = (
            ov[:, :MLA_V] / ov[:, MLA_V:MLA_V + 1]).astype(BF16)


def _attention(q, k, v, batch, seq, heads):
    t = q.shape[1]
    sub, hp = ATTN_SUB, ATTN_HEADS_PER_STEP
    slab = lambda w: pl.BlockSpec((hp, seq, w), lambda b, h: (h, b, 0))
    return pl.pallas_call(
        functools.partial(_attn_kernel, sub=sub, heads_per_step=hp),
        grid=(batch, heads // hp),
        in_specs=[slab(MLA_HEAD_PAD), slab(MLA_HEAD_PAD), slab(MLA_V)],
        out_specs=pl.BlockSpec((seq, hp * MLA_V), lambda b, h: (b, h)),
        out_shape=jax.ShapeDtypeStruct((t, heads * MLA_V), BF16),
        scratch_shapes=[pltpu.VMEM((hp, seq, 2 * MLA_V), BF16)],
        compiler_params=_cparams(("parallel", "parallel"), ATTN_VMEM_MIB),
        name="mla_attention",
    )(q, k, v)


def _mla_weights(w_in, w_uq, w_ukv, heads):
    q_lora = w_uq.shape[0]
    kv_lora = w_ukv.shape[0]
    wc = jnp.pad(w_in, [(0, 0), (0, LANES - MLA_ROPE)])
    uq = w_uq.reshape(q_lora, heads, MLA_DQK)
    wuq = jnp.pad(uq, [(0, 0), (0, 0), (0, MLA_HEAD_PAD - MLA_DQK)]).reshape(q_lora, heads * MLA_HEAD_PAD)
    ukv = w_ukv.reshape(kv_lora, heads, MLA_NOPE + MLA_V)
    wukv = jnp.concatenate([
        ukv[..., :MLA_NOPE].reshape(kv_lora, heads * MLA_NOPE),
        ukv[..., MLA_NOPE:].reshape(kv_lora, heads * MLA_V),
    ], axis=-1)
    return [w.astype(BF16) for w in (wc, wuq, wukv)]


def kernel(x, c, positions, norm_w, ada_w, ada_b, ret_w_in, ret_decay_logit_fwd,
           ret_decay_logit_bwd, ret_gn_w, ret_w_out, mla_w_in, mla_q_norm, mla_w_uq,
           mla_kv_norm, mla_w_ukv, mla_w_out, mlp_w1, mlp_w2):
    batch, seq, d = x.shape
    depth = norm_w.shape[0]
    assert depth == 2
    t = batch * seq
    ret_heads = ret_decay_logit_fwd.shape[1]
    mla_heads = mla_w_out.shape[1] // MLA_V

    mod = _ada_mod(c, ada_w, ada_b).reshape(depth, batch, 1, N_ADA * d)
    nw = norm_w.reshape(depth, 4, 1, d)
    inv_r, inv_m, sign_m = _rope_freqs()
    pos = positions.reshape(t, 1)
    x2 = x.reshape(t, d)

    qk, vg = _ret_proj(x2, nw, mod, 0, ret_w_in, pos, inv_r,
                       ret_decay_logit_fwd[0], ret_decay_logit_bwd[0], seq, ret_heads)
    y = _retention(qk, vg, ret_decay_logit_fwd[0], ret_decay_logit_bwd[0], ret_gn_w[0],
                   batch, seq, ret_heads)
    x2, h = _mixer_mlp(y, ret_w_out, x2, mlp_w1, mlp_w2, nw, mod, 0, seq, True)

    wc, wuq, wukv = _mla_weights(mla_w_in[0], mla_w_uq[0], mla_w_ukv[0], mla_heads)
    q, k, v = _mla_proj(h, wc, mla_q_norm[0], mla_kv_norm[0], wuq, wukv, pos, inv_m, sign_m,
                        mla_heads)
    o = _attention(q, k, v, batch, seq, mla_heads)
    (x2,) = _mixer_mlp(o, mla_w_out, x2, mlp_w1, mlp_w2, nw, mod, 1, seq, False)
    return x2.reshape(batch, seq, d)
```

```python
import functools

import numpy as np
import jax
import jax.numpy as jnp
from jax import lax
from jax.experimental import pallas as pl
from jax.experimental.pallas import tpu as pltpu

F32 = jnp.float32
BF16 = jnp.bfloat16

RMS_EPS = 1e-6
GN_EPS = 1e-5
ROPE_THETA = 10000.0
N_ADA = 6

RET_DK = 256
RET_DV = 512
RET_CHUNK = 256

MLA_NOPE = 128
MLA_ROPE = 64
MLA_V = 128
MLA_DQK = MLA_NOPE + MLA_ROPE
MLA_HEAD_PAD = 256

LANES = 128
MIB = 1024 * 1024

ADA_TN, ADA_VMEM_MIB = 1024, 32
RET_PROJ_TM, RET_PROJ_GROUPS, RET_PROJ_TN, RET_PROJ_VMEM_MIB = 512, 2, 1024, 56
RETENTION_VMEM_MIB = 48
MLP_TF = 1024
MLP_NEXT_TM, MLP_NEXT_GROUPS, MLP_NEXT_VMEM_MIB = 512, 2, 56
MLP_LAST_TM, MLP_LAST_GROUPS, MLP_LAST_VMEM_MIB = 1024, 4, 58
MLA_PROJ_TM, MLA_PROJ_GROUPS, MLA_PROJ_VMEM_MIB = 1024, 2, 48
ATTN_SUB, ATTN_HEADS_PER_STEP, ATTN_VMEM_MIB = 256, 2, 48
ATTN_KEY_CHUNK = 512
WEIGHT_STAGE = (2, 512, 1024)


def _cparams(semantics, vmem_mib):
    return pltpu.CompilerParams(dimension_semantics=semantics,
                                vmem_limit_bytes=vmem_mib * MIB)


def _rms(t, w):
    return t * lax.rsqrt(jnp.mean(t * t, axis=-1, keepdims=True) + RMS_EPS) * w


def _dot(a, b):
    return jnp.dot(a, b, preferred_element_type=F32)


def _dot_nt(a, b):
    return lax.dot_general(a, b, (((1,), (1,)), ((), ())), preferred_element_type=F32)


def _dot_tn(a, b):
    return lax.dot_general(a, b, (((0,), (0,)), ((), ())), preferred_element_type=F32)


def _ada_kernel(c_ref, w_ref, b_ref, o_ref):
    c_act = jax.nn.silu(c_ref[...])
    o_ref[...] = _dot(c_act.astype(BF16), w_ref[...].astype(BF16)) + b_ref[...]


def _ada_mod(c, ada_w, ada_b):
    depth, d, n = ada_w.shape
    b = c.shape[0]
    tn = ADA_TN
    return pl.pallas_call(
        _ada_kernel,
        grid=(depth, n // tn),
        in_specs=[
            pl.BlockSpec((b, d), lambda l, j: (0, 0)),
            pl.BlockSpec((None, d, tn), lambda l, j: (l, 0, j)),
            pl.BlockSpec((None, 1, tn), lambda l, j: (l, 0, j)),
        ],
        out_specs=pl.BlockSpec((None, b, tn), lambda l, j: (l, 0, j)),
        out_shape=jax.ShapeDtypeStruct((depth, b, n), F32),
        compiler_params=_cparams(("arbitrary", "arbitrary"), ADA_VMEM_MIB),
        name="ada_mod",
    )(c, ada_w, ada_b.reshape(depth, 1, n))


def _rope_freqs():
    half_m = MLA_ROPE // 2
    assert RET_DK // 2 == LANES
    inv_r = ROPE_THETA ** (-jnp.arange(0, RET_DK, 2, dtype=F32) / RET_DK)
    inv_m = ROPE_THETA ** (-jnp.arange(0, MLA_ROPE, 2, dtype=F32) / MLA_ROPE)
    inv_m = jnp.tile(inv_m, LANES // half_m)
    sign_m = jnp.concatenate([-jnp.ones((half_m,), F32), jnp.ones((half_m,), F32),
                              jnp.zeros((LANES - MLA_ROPE,), F32)])
    return inv_r.reshape(1, LANES), inv_m.reshape(1, LANES), sign_m.reshape(1, LANES)


def _ret_proj_kernel(x_ref, nw_ref, sc_ref, sh_ref, w_hbm, pos_ref, inv_ref, lf_ref, lb_ref,
                     qk_ref, vg_ref, w_ref, stage, sem, *, heads, k_scale, splits, tn):
    @pl.when(pl.program_id(0) == 0)
    def _():
        _stage_weight(w_hbm.at[0], w_ref, stage, sem)

    rows_per = x_ref.shape[0] // splits
    hk, hv = heads * RET_DK, heads * RET_DV
    half = RET_DK // 2
    local = lax.broadcasted_iota(jnp.int32, (rows_per, 1), 0).astype(F32)
    sdec_f = jnp.exp((rows_per - 1.0 - local) * jax.nn.log_sigmoid(lf_ref[...]))
    sdec_b = jnp.exp(local * jax.nn.log_sigmoid(lb_ref[...]))

    def rows(r):
        return slice(r * rows_per, (r + 1) * rows_per)

    pre_gain = nw_ref[...] * (1.0 + sc_ref[...])

    def prologue(r):
        h = (_rms(x_ref[rows(r), :], pre_gain) + sh_ref[...]).astype(BF16)
        ang = pos_ref[rows(r), :].astype(F32) * inv_ref[...]
        return h, jnp.cos(ang), jnp.sin(ang)

    def emit(r, col, p, cos, sin):
        if col >= 2 * hk:
            is_g = col >= 2 * hk + hv
            for g in range(tn // RET_DV):
                hd = (col - 2 * hk - is_g * hv) // RET_DV + g
                blk = p[:, g * RET_DV:(g + 1) * RET_DV]
                vg_ref[is_g * heads + hd, rows(r), :] = (jax.nn.silu(blk) if is_g else blk).astype(BF16)
        else:
            is_k = col >= hk
            if is_k:
                cos, sin = cos * k_scale, sin * k_scale
            for g in range(tn // RET_DK):
                hd = (col - is_k * hk) // RET_DK + g
                t1 = p[:, g * RET_DK:g * RET_DK + half]
                t2 = p[:, g * RET_DK + half:(g + 1) * RET_DK]
                r1 = t1 * cos - t2 * sin
                r2 = t1 * sin + t2 * cos
                qk_ref[is_k * heads + hd, rows(r), :half] = r1.astype(BF16)
                qk_ref[is_k * heads + hd, rows(r), half:] = r2.astype(BF16)
                if is_k:
                    for slab, dec in ((2 * heads + hd, sdec_f), (3 * heads + hd, sdec_b)):
                        d = dec[:, hd:hd + 1]
                        qk_ref[slab, rows(r), :half] = (r1 * d).astype(BF16)
                        qk_ref[slab, rows(r), half:] = (r2 * d).astype(BF16)

    order = ([c for c in range(0, 2 * hk, tn)] + [c for c in range(2 * hk + hv, 2 * hk + 2 * hv, tn)]
             + [c for c in range(2 * hk, 2 * hk + hv, tn)])
    cur = prologue(0)
    for r in range(splits):
        h, cos, sin = cur
        for idx, col in enumerate(order):
            p = _dot(h, w_ref[:, col:col + tn])
            if idx == 0 and r + 1 < splits:
                cur = prologue(r + 1)
            emit(r, col, p, cos, sin)


def _ret_proj(x2, nw, mod, layer, w_in, pos, inv_r, logit_f, logit_b, seq, heads):
    t, d = x2.shape
    n = w_in.shape[2]
    hk, hv = heads * RET_DK, heads * RET_DV
    tm, splits, tn = RET_PROJ_TM, RET_PROJ_GROUPS, RET_PROJ_TN
    assert hk % tn == 0 and hv % tn == 0 and tm // splits == RET_CHUNK and seq % tm == 0
    per_b = seq // tm
    vec = lambda k: pl.BlockSpec((None, None, 1, d), lambda i: (layer, i // per_b, 0, k))
    return pl.pallas_call(
        functools.partial(_ret_proj_kernel, heads=heads, k_scale=RET_DK ** -0.5, splits=splits, tn=tn),
        grid=(t // tm,),
        in_specs=[
            pl.BlockSpec((tm, d), lambda i: (i, 0)),
            pl.BlockSpec((None, None, 1, d), lambda i: (layer, 0, 0, 0)),
            vec(1), vec(0),
            pl.BlockSpec(memory_space=pl.ANY),
            pl.BlockSpec((tm, 1), lambda i: (i, 0)),
            _resident((1, LANES)),
            _resident((1, heads)), _resident((1, heads)),
        ],
        out_specs=[pl.BlockSpec((4 * heads, tm, RET_DK), lambda i: (0, i, 0)),
                   pl.BlockSpec((2 * heads, tm, RET_DV), lambda i: (0, i, 0))],
        out_shape=[jax.ShapeDtypeStruct((4 * heads, t, RET_DK), BF16),
                   jax.ShapeDtypeStruct((2 * heads, t, RET_DV), BF16)],
        scratch_shapes=[pltpu.VMEM((d, n), BF16), pltpu.VMEM(WEIGHT_STAGE, F32),
                        pltpu.SemaphoreType.DMA((2,))],
        compiler_params=_cparams(("arbitrary",), RET_PROJ_VMEM_MIB),
        name="ret_proj",
    )(x2, nw, mod, mod, w_in, pos, inv_r, logit_f.reshape(1, heads), logit_b.reshape(1, heads))


def _retention_kernel(lf_ref, lb_ref, q_ref, k_ref, kf_ref, kb_ref, v_ref, sg_ref, gn_ref, o_ref,
                      sf_all, sb_all):
    c = RET_CHUNK
    n = q_ref.shape[0] // c
    lg_f = jax.nn.log_sigmoid(lf_ref[...].astype(F32))
    lg_b = jax.nn.log_sigmoid(lb_ref[...].astype(F32))
    row = lax.broadcasted_iota(jnp.int32, (c, c), 0).astype(F32)
    col = lax.broadcasted_iota(jnp.int32, (c, c), 1).astype(F32)
    pos = lax.broadcasted_iota(jnp.int32, (c, 1), 0).astype(F32)
    diff = row - col
    mask_f = diff >= 0
    mask_b = diff < 0
    decay = jnp.where(mask_f, jnp.exp(jnp.where(mask_f, diff * lg_f, 0.0)),
                      jnp.exp(jnp.where(mask_b, -diff * lg_b, 0.0)))
    cross_f = jnp.exp((pos + 1.0) * lg_f)
    cross_b = jnp.exp((c - pos) * lg_b)
    cdec_f = jnp.exp(c * lg_f)
    cdec_b = jnp.exp(c * lg_b)
    gn_w = gn_ref[...]

    def rows(ci):
        return slice(ci * c, (ci + 1) * c)

    sf = jnp.zeros(sf_all.shape[1:], F32)
    sb = jnp.zeros(sb_all.shape[1:], F32)
    for i in range(n):
        cb = n - 1 - i
        sf_all[i] = sf.astype(BF16)
        sb_all[cb] = sb.astype(BF16)
        if i + 1 < n:
            sf = sf * cdec_f + _dot_tn(kf_ref[rows(i), :], v_ref[rows(i), :])
            sb = sb * cdec_b + _dot_tn(kb_ref[rows(cb), :], v_ref[rows(cb), :])

    for ci in range(n):
        r = rows(ci)
        qi, vi = q_ref[r, :], v_ref[r, :]
        s = _dot_nt(qi, k_ref[r, :]) * decay
        y = (_dot(s.astype(BF16), vi) + _dot(qi, sf_all[ci]) * cross_f
             + _dot(qi, sb_all[ci]) * cross_b)
        mu = jnp.mean(y, axis=-1, keepdims=True)
        yc = y - mu
        var = jnp.mean(yc * yc, axis=-1, keepdims=True)
        yn = yc * lax.rsqrt(var + GN_EPS) * gn_w
        o_ref[r, :] = (sg_ref[r, :].astype(F32) * yn).astype(BF16)


def _retention(qk, vg, logit_f, logit_b, gn_w, batch, seq, heads):
    t = qk.shape[1]
    hv = heads * RET_DV
    scalar = pl.BlockSpec((None, 1, 1), lambda b, h: (h, 0, 0))
    slab = lambda m, w: pl.BlockSpec((None, seq, w), lambda b, h: (m * heads + h, b, 0))
    return pl.pallas_call(
        _retention_kernel,
        grid=(batch, heads),
        in_specs=[
            scalar, scalar,
            *[slab(m, RET_DK) for m in range(4)],
            slab(0, RET_DV), slab(1, RET_DV),
            pl.BlockSpec((1, RET_DV), lambda b, h: (0, h)),
        ],
        out_specs=pl.BlockSpec((seq, RET_DV), lambda b, h: (b, h)),
        out_shape=jax.ShapeDtypeStruct((t, hv), BF16),
        scratch_shapes=[
            pltpu.VMEM((seq // RET_CHUNK, RET_DK, RET_DV), BF16),
            pltpu.VMEM((seq // RET_CHUNK, RET_DK, RET_DV), BF16),
        ],
        compiler_params=_cparams(("parallel", "parallel"), RETENTION_VMEM_MIB),
        name="retention",
    )(logit_f.reshape(heads, 1, 1), logit_b.reshape(heads, 1, 1), *([qk] * 4), vg, vg,
      gn_w.reshape(1, hv))


def _stage_weight(w_hbm, w_scr, stage, sem):
    rows, cols = w_scr.shape
    _, sr, sc = stage.shape
    tiles = [(r0, c0) for r0 in range(0, rows, sr) for c0 in range(0, cols, sc)]

    def copy(i):
        r0, c0 = tiles[i]
        return pltpu.make_async_copy(w_hbm.at[pl.ds(r0, sr), pl.ds(c0, sc)],
                                     stage.at[i % 2], sem.at[i % 2])

    copy(0).start()
    for i, (r0, c0) in enumerate(tiles):
        if i + 1 < len(tiles):
            copy(i + 1).start()
        copy(i).wait()
        w_scr[r0:r0 + sr, c0:c0 + sc] = stage[i % 2].astype(BF16)


def _mixer_mlp_kernel(*refs, emit_next, splits, tf, layer):
    if emit_next:
        (a_ref, wo_hbm, x_ref, ga_ref, nw1_ref, nw2_ref, scm_ref, shm_ref, w1_hbm, w2_hbm,
         gm_ref, nw3_ref, nwn_ref, scn_ref, shn_ref, xo_ref, hn_ref,
         wo_ref, w1_ref, w2_ref, stage, sem) = refs
    else:
        (a_ref, wo_hbm, x_ref, ga_ref, nw1_ref, nw2_ref, scm_ref, shm_ref, w1_hbm, w2_hbm,
         gm_ref, nw3_ref, xo_ref, wo_ref, w1_ref, w2_ref, stage, sem) = refs

    @pl.when(pl.program_id(0) == 0)
    def _():
        _stage_weight(wo_hbm.at[0], wo_ref, stage, sem)
        _stage_weight(w1_hbm.at[layer], w1_ref, stage, sem)
        _stage_weight(w2_hbm.at[layer], w2_ref, stage, sem)

    rows_per = x_ref.shape[0] // splits
    n_chunks = w1_ref.shape[1] // tf

    def rows(r):
        return slice(r * rows_per, (r + 1) * rows_per)

    post_mix = ga_ref[...] * nw1_ref[...]
    pre_mlp = nw2_ref[...] * (1.0 + scm_ref[...])
    post_mlp = gm_ref[...] * nw3_ref[...]
    if emit_next:
        pre_next = nwn_ref[...] * (1.0 + scn_ref[...])

    def prologue(r):
        y = _dot(a_ref[rows(r), :], wo_ref[...])
        x1 = x_ref[rows(r), :] + _rms(y, post_mix)
        h = (_rms(x1, pre_mlp) + shm_ref[...]).astype(BF16)
        return x1, h

    def epilogue(r, x1, acc):
        x2 = x1 + _rms(acc, post_mlp)
        xo_ref[rows(r), :] = x2
        if emit_next:
            hn_ref[rows(r), :] = (_rms(x2, pre_next) + shn_ref[...]).astype(BF16)

    cur = prologue(0)
    done = None
    for r in range(splits):
        x1, h = cur
        acc = None
        for j in range(n_chunks):
            cols = slice(j * tf, (j + 1) * tf)
            u = jnp.square(jnp.maximum(_dot(h, w1_ref[:, cols]), 0.0)).astype(BF16)
            part = _dot(u, w2_ref[cols, :])
            acc = part if acc is None else acc + part
            if j == 0 and r + 1 < splits:
                cur = prologue(r + 1)
            if j == min(1, n_chunks - 1) and done is not None:
                epilogue(*done)
                done = None
        done = (r, x1, acc)
    epilogue(*done)


def _resident(shape):
    return pl.BlockSpec(shape, lambda *_: (0,) * len(shape), pipeline_mode=pl.Buffered(1))


def _mixer_mlp(a, w_out, x2, w1, w2, nw, mod, layer, seq, emit_next):
    t, d = x2.shape
    kd = a.shape[1]
    dff = w1.shape[2]
    tf = MLP_TF
    hbm = pl.BlockSpec(memory_space=pl.ANY)
    if emit_next:
        tm, splits, vmem_mib = MLP_NEXT_TM, MLP_NEXT_GROUPS, MLP_NEXT_VMEM_MIB
    else:
        tm, splits, vmem_mib = MLP_LAST_TM, MLP_LAST_GROUPS, MLP_LAST_VMEM_MIB
    per_b = seq // tm
    vec = lambda l, k: pl.BlockSpec((None, None, 1, d), lambda i: (l, i // per_b, 0, k))
    nvec = lambda l, k: pl.BlockSpec((None, None, 1, d), lambda i: (l, k, 0, 0))
    row = pl.BlockSpec((tm, d), lambda i: (i, 0))
    in_specs = [
        pl.BlockSpec((tm, kd), lambda i: (i, 0)), hbm, row,
        vec(layer, 2), nvec(layer, 1), nvec(layer, 2), vec(layer, 4), vec(layer, 3),
        hbm, hbm, vec(layer, 5), nvec(layer, 3),
    ]
    args = [a, w_out, x2, mod, nw, nw, mod, mod, w1, w2, mod, nw]
    out_specs = [row]
    out_shape = [jax.ShapeDtypeStruct((t, d), F32)]
    if emit_next:
        in_specs += [nvec(layer + 1, 0), vec(layer + 1, 1), vec(layer + 1, 0)]
        args += [nw, mod, mod]
        out_specs.append(row)
        out_shape.append(jax.ShapeDtypeStruct((t, d), BF16))
    return pl.pallas_call(
        functools.partial(_mixer_mlp_kernel, emit_next=emit_next, splits=splits, tf=tf,
                          layer=layer),
        grid=(t // tm,),
        in_specs=in_specs,
        out_specs=out_specs,
        out_shape=out_shape,
        scratch_shapes=[pltpu.VMEM((kd, d), BF16), pltpu.VMEM((d, dff), BF16),
                        pltpu.VMEM((dff, d), BF16), pltpu.VMEM(WEIGHT_STAGE, F32),
                        pltpu.SemaphoreType.DMA((2,))],
        compiler_params=_cparams(("arbitrary",), vmem_mib),
        name="mixer_mlp_next" if emit_next else "mixer_mlp_last",
    )(*args)


def _mla_proj_kernel(h_ref, wc_ref, qn_ref, kvn_ref, wuq_ref, wukv_ref, pos_ref, inv_ref,
                     sign_ref, q_ref, k_ref, v_ref, *, heads, splits):
    q_lora = qn_ref.shape[1]
    kv_lora = kvn_ref.shape[1]
    nope_w = heads * MLA_NOPE
    rows_per = h_ref.shape[0] // splits
    lane = lax.broadcasted_iota(jnp.int32, (1, LANES), 1)
    first_half = lane < MLA_ROPE // 2
    pair = rows_per // 2

    def compress(r):
        rows = slice(r * rows_per, (r + 1) * rows_per)
        c_all = _dot(h_ref[rows, :], wc_ref[...])
        cq = _rms(c_all[:, :q_lora], qn_ref[...]).astype(BF16)
        ckv = _rms(c_all[:, q_lora:q_lora + kv_lora], kvn_ref[...]).astype(BF16)
        pos_a = pos_ref[r * rows_per:r * rows_per + pair, :]
        pos_b = pos_ref[r * rows_per + pair:(r + 1) * rows_per, :]
        ang = jnp.where(lane < MLA_ROPE, pos_a, pos_b).astype(F32) * inv_ref[...]
        cos2 = jnp.cos(ang)
        sin2 = jnp.sin(ang)
        cos = jnp.concatenate([cos2, pltpu.roll(cos2, MLA_ROPE, 1)], axis=0)
        sin = jnp.concatenate([sin2, pltpu.roll(sin2, MLA_ROPE, 1)], axis=0) * sign_ref[...]
        return cq, ckv, c_all[:, q_lora + kv_lora:], cos, sin

    cur = compress(0)
    for r in range(splits):
        rows = slice(r * rows_per, (r + 1) * rows_per)
        cq, ckv, kr_raw, cos, sin = cur

        def rotary(t, cos=cos, sin=sin):
            partner = jnp.where(first_half, pltpu.roll(t, LANES - MLA_ROPE // 2, 1),
                                pltpu.roll(t, MLA_ROPE // 2, 1))
            return t * cos + partner * sin

        kr = rotary(kr_raw).astype(BF16)
        for hd in range(heads):
            lo = hd * MLA_HEAD_PAD
            if hd % 2 == 0:
                kn = _dot(ckv, wukv_ref[:, hd * LANES:(hd + 2) * LANES])
                vv = _dot(ckv, wukv_ref[:, nope_w + hd * MLA_V:nope_w + (hd + 2) * MLA_V])
            if hd == 2 and r + 1 < splits:
                cur = compress(r + 1)
            qh = _dot(cq, wuq_ref[:, lo:lo + MLA_HEAD_PAD])
            q_ref[hd, rows, :LANES] = qh[:, :LANES].astype(BF16)
            q_ref[hd, rows, LANES:] = rotary(qh[:, LANES:]).astype(BF16)
            k_ref[hd, rows, :LANES] = kn[:, (hd % 2) * LANES:(hd % 2 + 1) * LANES].astype(BF16)
            k_ref[hd, rows, LANES:] = kr
            v_ref[hd, rows, :] = vv[:, (hd % 2) * MLA_V:(hd % 2 + 1) * MLA_V].astype(BF16)


def _mla_proj(h, wc, q_norm, kv_norm, wuq, wukv, pos, inv_m, sign_m, heads):
    t, d = h.shape
    tm, splits = MLA_PROJ_TM, MLA_PROJ_GROUPS
    qn = q_norm.reshape(1, -1) * float(MLA_DQK ** -0.5 * np.log2(np.e))
    kvn = kv_norm.reshape(1, -1)
    row = lambda w: pl.BlockSpec((tm, w), lambda i: (i, 0))
    return pl.pallas_call(
        functools.partial(_mla_proj_kernel, heads=heads, splits=splits),
        grid=(t // tm,),
        in_specs=[row(d), _resident(wc.shape), _resident(qn.shape), _resident(kvn.shape),
                  _resident(wuq.shape), _resident(wukv.shape), row(1), _resident((1, LANES)),
                  _resident((1, LANES))],
        out_specs=[pl.BlockSpec((heads, tm, w), lambda i: (0, i, 0))
                   for w in (MLA_HEAD_PAD, MLA_HEAD_PAD, MLA_V)],
        out_shape=[jax.ShapeDtypeStruct((heads, t, w), BF16)
                   for w in (MLA_HEAD_PAD, MLA_HEAD_PAD, MLA_V)],
        compiler_params=_cparams(("parallel",), MLA_PROJ_VMEM_MIB),
        name="mla_proj",
    )(h, wc, qn, kvn, wuq, wukv, pos, inv_m, sign_m)


def _attn_kernel(q_ref, k_ref, v_ref, o_ref, v1_scr, *, sub, heads_per_step):
    seq = q_ref.shape[1]
    n_sub = seq // sub
    for hh in range(heads_per_step):
        v1_scr[hh, :, :MLA_V] = v_ref[hh]
        v1_scr[hh, :, MLA_V:] = jnp.ones((seq, MLA_V), BF16)

    kc = ATTN_KEY_CHUNK
    for i in range(heads_per_step * n_sub):
        hh, r = divmod(i, n_sub)
        q = q_ref[hh, r * sub:(r + 1) * sub, :]
        m = acc = None
        for k0 in range(0, seq, kc):
            s = _dot_nt(q, k_ref[hh, k0:k0 + kc, :])
            m_chunk = jnp.max(s, axis=-1, keepdims=True)
            m_new = m_chunk if m is None else jnp.maximum(m, m_chunk)
            pv = _dot(jnp.exp2(s - m_new).astype(BF16), v1_scr[hh, k0:k0 + kc, :])
            acc = pv if acc is None else acc * jnp.exp2(m - m_new) + pv
            m = m_new
        o_ref[r * sub:(r + 1) * sub, hh * MLA_V:(hh + 1) * MLA_V] = (
            acc[:, :MLA_V] / acc[:, MLA_V:MLA_V + 1]).astype(BF16)


def _attention(q, k, v, batch, seq, heads):
    t = q.shape[1]
    sub, hp = ATTN_SUB, ATTN_HEADS_PER_STEP
    slab = lambda w: pl.BlockSpec((hp, seq, w), lambda b, h: (h, b, 0))
    return pl.pallas_call(
        functools.partial(_attn_kernel, sub=sub, heads_per_step=hp),
        grid=(batch, heads // hp),
        in_specs=[slab(MLA_HEAD_PAD), slab(MLA_HEAD_PAD), slab(MLA_V)],
        out_specs=pl.BlockSpec((seq, hp * MLA_V), lambda b, h: (b, h)),
        out_shape=jax.ShapeDtypeStruct((t, heads * MLA_V), BF16),
        scratch_shapes=[pltpu.VMEM((hp, seq, 2 * MLA_V), BF16)],
        compiler_params=_cparams(("parallel", "parallel"), ATTN_VMEM_MIB),
        name="mla_attention",
    )(q, k, v)


def _mla_weights(w_in, w_uq, w_ukv, heads):
    q_lora = w_uq.shape[0]
    kv_lora = w_ukv.shape[0]
    wc = jnp.pad(w_in, [(0, 0), (0, LANES - MLA_ROPE)])
    uq = w_uq.reshape(q_lora, heads, MLA_DQK)
    wuq = jnp.pad(uq, [(0, 0), (0, 0), (0, MLA_HEAD_PAD - MLA_DQK)]).reshape(q_lora, heads * MLA_HEAD_PAD)
    ukv = w_ukv.reshape(kv_lora, heads, MLA_NOPE + MLA_V)
    wukv = jnp.concatenate([
        ukv[..., :MLA_NOPE].reshape(kv_lora, heads * MLA_NOPE),
        ukv[..., MLA_NOPE:].reshape(kv_lora, heads * MLA_V),
    ], axis=-1)
    return [w.astype(BF16) for w in (wc, wuq, wukv)]


def kernel(x, c, positions, norm_w, ada_w, ada_b, ret_w_in, ret_decay_logit_fwd,
           ret_decay_logit_bwd, ret_gn_w, ret_w_out, mla_w_in, mla_q_norm, mla_w_uq,
           mla_kv_norm, mla_w_ukv, mla_w_out, mlp_w1, mlp_w2):
    batch, seq, d = x.shape
    depth = norm_w.shape[0]
    assert depth == 2
    t = batch * seq
    ret_heads = ret_decay_logit_fwd.shape[1]
    mla_heads = mla_w_out.shape[1] // MLA_V

    mod = _ada_mod(c, ada_w, ada_b).reshape(depth, batch, 1, N_ADA * d)
    nw = norm_w.reshape(depth, 4, 1, d)
    inv_r, inv_m, sign_m = _rope_freqs()
    pos = positions.reshape(t, 1)
    x2 = x.reshape(t, d)

    qk, vg = _ret_proj(x2, nw, mod, 0, ret_w_in, pos, inv_r,
                       ret_decay_logit_fwd[0], ret_decay_logit_bwd[0], seq, ret_heads)
    y = _retention(qk, vg, ret_decay_logit_fwd[0], ret_decay_logit_bwd[0], ret_gn_w[0],
                   batch, seq, ret_heads)
    x2, h = _mixer_mlp(y, ret_w_out, x2, mlp_w1, mlp_w2, nw, mod, 0, seq, True)

    wc, wuq, wukv = _mla_weights(mla_w_in[0], mla_w_uq[0], mla_w_ukv[0], mla_heads)
    q, k, v = _mla_proj(h, wc, mla_q_norm[0], mla_kv_norm[0], wuq, wukv, pos, inv_m, sign_m,
                        mla_heads)
    o = _attention(q, k, v, batch, seq, mla_heads)
    (x2,) = _mixer_mlp(o, mla_w_out, x2, mlp_w1, mlp_w2, nw, mod, 1, seq, False)
    return x2.reshape(batch, seq, d)
```
